```python
import math
import jax
import jax.numpy as jnp
from jax import lax
import numpy as np


D_MODEL = 1024
BATCH = 16
SEQ = 4096
DEPTH = 2

HEAD_DIM = 64
SB_HEADS = 6
RET_HEADS = 6
SB_WIDTH = SB_HEADS * HEAD_DIM
RET_WIDTH = RET_HEADS * HEAD_DIM
S5_WIDTH = D_MODEL - SB_WIDTH - RET_WIDTH
S5_GROUP_CH = 16
S5_GROUPS = S5_WIDTH // S5_GROUP_CH
S5_STATE = 64
D_MIX = SB_WIDTH + RET_WIDTH + S5_WIDTH
D_IN = 3 * SB_WIDTH + 4 * RET_WIDTH + S5_WIDTH
D_FF = ((8 * D_MODEL + 3 * 256 - 1) // (3 * 256)) * 256
CHUNK = 128
N_META = 16
PAD_FRONT = CHUNK - N_META
ROPE_BASE = 10000.0
RMS_EPS = 1e-6

kernel_name = "hymba_sb_retention_s5_hybrid"


def rmsnorm(x, g):
    x32 = x.astype(jnp.float32)
    y = x32 * lax.rsqrt(jnp.mean(x32 * x32, axis=-1, keepdims=True) + RMS_EPS)
    return y.astype(x.dtype) * g


def to_heads(t, n_heads):
    b, l, _ = t.shape
    return t.reshape(b, l, n_heads, HEAD_DIM).transpose(0, 2, 1, 3)


def from_heads(t):
    b, h, l, d = t.shape
    return t.transpose(0, 2, 1, 3).reshape(b, l, h * d)


def rotary(t, pos):
    half = HEAD_DIM // 2
    inv = ROPE_BASE ** (-jnp.arange(half, dtype=jnp.float32) / half)
    ang = pos[:, None] * inv[None, :]
    cos = jnp.cos(ang).astype(t.dtype)
    sin = jnp.sin(ang).astype(t.dtype)
    t1, t2 = t[..., :half], t[..., half:]
    return jnp.concatenate([t1 * cos - t2 * sin, t1 * sin + t2 * cos], axis=-1)


def stick_breaking(q, k, v):
    L = q.shape[2]
    scale = HEAD_DIM ** -0.5
    outs = []
    for blk in range(L // CHUNK):
        end = (blk + 1) * CHUNK
        qb = q[:, :, blk * CHUNK:end]
        kb = k[:, :, :end]
        vb = v[:, :, :end]
        z = jnp.einsum('bhqd,bhkd->bhqk', qb, kb).astype(jnp.float32) * scale
        t_idx = blk * CHUNK + jnp.arange(CHUNK)
        s_idx = jnp.arange(end)
        mask = (s_idx[None, :] < t_idx[:, None]) & (s_idx[None, :] >= PAD_FRONT)
        log1m = jnp.where(mask, jax.nn.log_sigmoid(-z), 0.0)
        between = lax.cumsum(log1m, axis=3, reverse=True) - log1m
        w = jnp.where(mask, jnp.exp(jax.nn.log_sigmoid(z) + between), 0.0)
        outs.append(jnp.einsum('bhqk,bhkd->bhqd', w.astype(vb.dtype), vb))
    return jnp.concatenate(outs, axis=2)


def retention(q, k, v, log_gamma):
    b, h, L, d = q.shape
    n = L // CHUNK
    k = k * (d ** -0.5)
    qc = q.reshape(b, h, n, CHUNK, d)
    kc = k.reshape(b, h, n, CHUNK, d)
    vc = v.reshape(b, h, n, CHUNK, d)
    i = jnp.arange(CHUNK, dtype=jnp.float32)
    diff = i[:, None] - i[None, :]
    decay_in = jnp.where(diff >= 0, jnp.exp(log_gamma[:, None, None] * jnp.maximum(diff, 0.0)), 0.0)
    scores = jnp.einsum('bhnid,bhnjd->bhnij', qc, kc) * decay_in[None, :, None]
    inner = jnp.einsum('bhnij,bhnje->bhnie', scores, vc.astype(jnp.float32))
    q_decay = jnp.exp(log_gamma[:, None] * (i + 1.0))
    k_decay = jnp.exp(log_gamma[:, None] * (CHUNK - 1.0 - i))
    chunk_decay = jnp.exp(log_gamma * CHUNK)
    xs = (jnp.moveaxis(qc.astype(jnp.float32) * q_decay[None, :, None, :, None], 2, 0),
          jnp.moveaxis(kc.astype(jnp.float32) * k_decay[None, :, None, :, None], 2, 0),
          jnp.moveaxis(vc.astype(jnp.float32), 2, 0))

    def step(state, inp):
        qn, kn, vn = inp
        cross = jnp.einsum('bhid,bhde->bhie', qn, state)
        state = state * chunk_decay[None, :, None, None] + jnp.einsum('bhjd,bhje->bhde', kn, vn)
        return state, cross

    s0 = jnp.zeros((b, h, d, d), jnp.float32)
    _, cross = lax.scan(step, s0, xs)
    o = inner + jnp.moveaxis(cross, 0, 2)
    return o.reshape(b, h, L, d)


def _complex_affine_combine(e1, e2):
    a1r, a1i, b1r, b1i = e1
    a2r, a2i, b2r, b2i = e2
    return (a2r * a1r - a2i * a1i,
            a2r * a1i + a2i * a1r,
            a2r * b1r - a2i * b1i + b2r,
            a2r * b1i + a2i * b1r + b2i)


def s5_mixer(u, lam_re, lam_im, log_dt, b_re, b_im, c_re, c_im, d_skip, w_glu):
    bsz, L, _ = u.shape
    f32 = jnp.float32
    u32 = u.astype(f32)
    ug = jnp.transpose(u32.reshape(bsz, L, S5_GROUPS, S5_GROUP_CH), (1, 0, 2, 3))
    lr = lam_re.astype(f32)
    li = lam_im.astype(f32)
    dt = jnp.exp(log_dt.astype(f32))[:, None]
    mag = jnp.exp(lr * dt)
    ar = mag * jnp.cos(li * dt)
    ai = mag * jnp.sin(li * dt)
    den = lr * lr + li * li
    fr = ((ar - 1.0) * lr + ai * li) / den
    fi = (ai * lr - (ar - 1.0) * li) / den
    br = b_re.astype(f32)
    bi = b_im.astype(f32)
    bbr = fr[..., None] * br - fi[..., None] * bi
    bbi = fr[..., None] * bi + fi[..., None] * br
    bu_r = jnp.einsum('lbgc,gpc->lbgp', ug, bbr)
    bu_i = jnp.einsum('lbgc,gpc->lbgp', ug, bbi)
    a_r = jnp.broadcast_to(ar[None, None], (L, 1, S5_GROUPS, S5_STATE))
    a_i = jnp.broadcast_to(ai[None, None], (L, 1, S5_GROUPS, S5_STATE))
    _, _, xr, xi = lax.associative_scan(_complex_affine_combine, (a_r, a_i, bu_r, bu_i), axis=0)
    y = (jnp.einsum('lbgp,gcp->lbgc', xr, c_re.astype(f32))
         - jnp.einsum('lbgp,gcp->lbgc', xi, c_im.astype(f32)))
    y = jnp.transpose(y, (1, 0, 2, 3)).reshape(bsz, L, S5_WIDTH) + d_skip.astype(f32) * u32
    y = jax.nn.gelu(y)
    return (y * jax.nn.sigmoid(y @ w_glu.astype(f32))).astype(u.dtype)


def setup_inputs(seed: int = 0) -> dict:
    key = jax.random.key(seed)
    ks = jax.random.split(key, 24)
    f32 = jnp.float32

    def nrm(k, shape, scale):
        return scale * jax.random.normal(k, shape, f32)

    return {
        'x': nrm(ks[0], (BATCH, SEQ, D_MODEL), 1.0),
        'meta_tokens': nrm(ks[1], (N_META, D_MODEL), 1.0),
        'norm1_g': 1.0 + nrm(ks[2], (DEPTH, D_MODEL), 0.02),
        'w_in': nrm(ks[3], (DEPTH, D_MODEL, D_IN), D_MODEL ** -0.5),
        'sb_q_g': 1.0 + nrm(ks[4], (DEPTH, HEAD_DIM), 0.02),
        'sb_k_g': 1.0 + nrm(ks[5], (DEPTH, HEAD_DIM), 0.02),
        'ret_q_g': 1.0 + nrm(ks[6], (DEPTH, HEAD_DIM), 0.02),
        'ret_k_g': 1.0 + nrm(ks[7], (DEPTH, HEAD_DIM), 0.02),
        'ret_out_g': 1.0 + nrm(ks[8], (DEPTH, RET_WIDTH), 0.02),
        's5_lam_re': -0.5 + nrm(ks[9], (DEPTH, S5_GROUPS, S5_STATE), 0.01),
        's5_lam_im': math.pi * jnp.arange(S5_STATE, dtype=f32) + nrm(ks[10], (DEPTH, S5_GROUPS, S5_STATE), 0.01),
        's5_log_dt': jax.random.uniform(ks[11], (DEPTH, S5_GROUPS), f32, math.log(1e-3), math.log(1e-1)),
        's5_b_re': nrm(ks[12], (DEPTH, S5_GROUPS, S5_STATE, S5_GROUP_CH), (2 * S5_GROUP_CH) ** -0.5),
        's5_b_im': nrm(ks[13], (DEPTH, S5_GROUPS, S5_STATE, S5_GROUP_CH), (2 * S5_GROUP_CH) ** -0.5),
        's5_c_re': nrm(ks[14], (DEPTH, S5_GROUPS, S5_GROUP_CH, S5_STATE), (2 * S5_STATE) ** -0.5),
        's5_c_im': nrm(ks[15], (DEPTH, S5_GROUPS, S5_GROUP_CH, S5_STATE), (2 * S5_STATE) ** -0.5),
        's5_d': nrm(ks[16], (DEPTH, S5_WIDTH), 1.0),
        's5_w_glu': nrm(ks[17], (DEPTH, S5_WIDTH, S5_WIDTH), S5_WIDTH ** -0.5),
        'w_out': nrm(ks[18], (DEPTH, D_MIX, D_MODEL), D_MIX ** -0.5),
        'norm2_g': 1.0 + nrm(ks[19], (DEPTH, D_MODEL), 0.02),
        'w_gate': nrm(ks[20], (DEPTH, D_MODEL, D_FF), D_MODEL ** -0.5),
        'w_up': nrm(ks[21], (DEPTH, D_MODEL, D_FF), D_MODEL ** -0.5),
        'w_down': nrm(ks[22], (DEPTH, D_FF, D_MODEL), D_FF ** -0.5),
    }


def reference(x, meta_tokens, norm1_g, w_in, sb_q_g, sb_k_g, ret_q_g, ret_k_g, ret_out_g,
              s5_lam_re, s5_lam_im, s5_log_dt, s5_b_re, s5_b_im, s5_c_re, s5_c_im, s5_d, s5_w_glu,
              w_out, norm2_g, w_gate, w_up, w_down):
    bsz = x.shape[0]
    L = PAD_FRONT + N_META + x.shape[1]
    pad = jnp.zeros((bsz, PAD_FRONT, D_MODEL), x.dtype)
    meta = jnp.broadcast_to(meta_tokens[None].astype(x.dtype), (bsz, N_META, D_MODEL))
    h = jnp.concatenate([pad, meta, x], axis=1)
    idx = jnp.arange(L)
    valid = (idx >= PAD_FRONT).astype(x.dtype)[None, :, None]
    pos = (idx - PAD_FRONT).astype(jnp.float32)
    log_gamma = jnp.log1p(-jnp.exp2(-5.0 - jnp.arange(RET_HEADS, dtype=jnp.float32)))
    splits = [SB_WIDTH, 2 * SB_WIDTH, 3 * SB_WIDTH,
              3 * SB_WIDTH + RET_WIDTH, 3 * SB_WIDTH + 2 * RET_WIDTH,
              3 * SB_WIDTH + 3 * RET_WIDTH, 3 * SB_WIDTH + 4 * RET_WIDTH]
    for l in range(DEPTH):
        hn = rmsnorm(h, norm1_g[l])
        proj = hn @ w_in[l]
        sq, sk, sv, rq, rk, rv, rg, u = jnp.split(proj, splits, axis=-1)
        q = rmsnorm(to_heads(sq, SB_HEADS), sb_q_g[l])
        k = rmsnorm(to_heads(sk, SB_HEADS), sb_k_g[l])
        sb_o = from_heads(stick_breaking(q, k, to_heads(sv, SB_HEADS)))
        q = rotary(rmsnorm(to_heads(rq, RET_HEADS), ret_q_g[l]), pos)
        k = rotary(rmsnorm(to_heads(rk, RET_HEADS), ret_k_g[l]), pos)
        ro = retention(q, k, to_heads(rv, RET_HEADS), log_gamma).transpose(0, 2, 1, 3)
        ro = rmsnorm(ro, ret_out_g[l].reshape(RET_HEADS, HEAD_DIM)).reshape(bsz, L, RET_WIDTH)
        ro = ro * jax.nn.silu(rg)
        so = s5_mixer(u, s5_lam_re[l], s5_lam_im[l], s5_log_dt[l], s5_b_re[l], s5_b_im[l],
                      s5_c_re[l], s5_c_im[l], s5_d[l], s5_w_glu[l])
        mix = jnp.concatenate([sb_o.astype(h.dtype), ro.astype(h.dtype), so.astype(h.dtype)], axis=-1)
        h = h + (mix @ w_out[l]).astype(h.dtype)
        hn = rmsnorm(h, norm2_g[l])
        h = h + ((jax.nn.silu(hn @ w_gate[l]) * (hn @ w_up[l])) @ w_down[l]).astype(h.dtype)
        h = h * valid
    return h[:, PAD_FRONT + N_META:]
```

```python
import functools
import math

import jax
import jax.numpy as jnp
from jax import lax
from jax.experimental import pallas as pl
from jax.experimental.pallas import tpu as pltpu

D_MODEL = 1024
HEAD_DIM = 64
SB_HEADS = 6
RET_HEADS = 6
SB_WIDTH = SB_HEADS * HEAD_DIM
RET_WIDTH = RET_HEADS * HEAD_DIM
S5_WIDTH = D_MODEL - SB_WIDTH - RET_WIDTH
S5_GROUP_CH = 16
S5_GROUPS = S5_WIDTH // S5_GROUP_CH
S5_STATE = 64
S5_NSTATE = S5_GROUPS * S5_STATE
D_IN = 3 * SB_WIDTH + 4 * RET_WIDTH + S5_WIDTH
D_FF = 2816
CHUNK = 128
N_META = 16
PAD_FRONT = CHUNK - N_META
ROPE_BASE = 10000.0
RMS_EPS = 1e-6

LANES = 128
HEAD_PAIRS = SB_HEADS // 2
VMEM_LIMIT_BYTES = 56 * 1024 * 1024

F32 = jnp.float32
BF16 = jnp.bfloat16


def _dot(a, b):
    return jnp.dot(a, b, preferred_element_type=F32)


def _dot_nt(a, b):
    return lax.dot_general(a, b, (((1,), (1,)), ((), ())), preferred_element_type=F32)


def _dot_tn(a, b):
    return lax.dot_general(a, b, (((0,), (0,)), ((), ())), preferred_element_type=F32)


def _split_dot(x, m):
    hi = x.astype(BF16)
    lo = (x - hi.astype(F32)).astype(BF16)
    return _dot(hi, m) + _dot(lo, m)


def _head_split(t, first_head):
    zero = jnp.zeros_like(t)
    return jnp.concatenate([jnp.where(first_head, t, zero), jnp.where(first_head, zero, t)], axis=0)


def _first_head_lanes():
    return lax.broadcasted_iota(jnp.int32, (1, LANES), 1) < HEAD_DIM


def _in_kernel(h_ref, g1_ref, w_ref, hg_ref, bd_ref, cos_ref, sin_ref,
               sq_ref, sk_ref, sv_ref, rq_ref, rk_ref, rv_ref, rg_ref, u_ref):
    x = h_ref[0]
    ms = jnp.mean(x * x, axis=-1, keepdims=True)
    hn = (x * lax.rsqrt(ms + RMS_EPS) * g1_ref[...]).astype(BF16)

    def proj(seg, width=SB_WIDTH):
        return _dot(hn, w_ref[:, seg * SB_WIDTH:seg * SB_WIDTH + width])

    def head_norm(p, gi):
        m = _split_dot(p * p, bd_ref[...])
        return p * lax.rsqrt(m + RMS_EPS) * hg_ref[gi:gi + 1, :]

    lane = lax.broadcasted_iota(jnp.int32, (1, LANES), 1)
    lower = (lane & (HEAD_DIM - 1)) < HEAD_DIM // 2
    cos = cos_ref[...]
    sin = sin_ref[...]

    def rotary(t):
        outs = []
        for c in range(HEAD_PAIRS):
            tc = t[:, c * LANES:(c + 1) * LANES]
            partner = jnp.where(lower, pltpu.roll(tc, LANES - HEAD_DIM // 2, 1),
                                pltpu.roll(tc, HEAD_DIM // 2, 1))
            outs.append(tc * cos + partner * sin)
        return jnp.concatenate(outs, axis=1)

    sq_ref[0] = (head_norm(proj(0), 0) * (HEAD_DIM ** -0.5)).astype(BF16)
    sk_ref[0] = head_norm(proj(1), 1).astype(BF16)
    sv_ref[0] = proj(2).astype(BF16)
    rq_ref[0] = rotary(head_norm(proj(3), 2))
    rk_ref[0] = rotary(head_norm(proj(4), 3))
    rv_ref[0] = proj(5).astype(BF16)
    rg_ref[0] = proj(6)
    u_ref[...] = proj(7, S5_WIDTH)


def _in_proj(h, g1, w_in, head_gains, bd, cos_t, sin_t, tm):
    bsz, L, _ = h.shape
    grid = (bsz, L // tm)
    row3 = lambda w: pl.BlockSpec((1, tm, w), lambda b, t: (b, t, 0))
    const = lambda shape: pl.BlockSpec(shape, lambda b, t: (0,) * len(shape), pipeline_mode=pl.Buffered(1))
    out_shapes = [
        jax.ShapeDtypeStruct((bsz, L, SB_WIDTH), BF16),
        jax.ShapeDtypeStruct((bsz, L, SB_WIDTH), BF16),
        jax.ShapeDtypeStruct((bsz, L, SB_WIDTH), BF16),
        jax.ShapeDtypeStruct((bsz, L, RET_WIDTH), F32),
        jax.ShapeDtypeStruct((bsz, L, RET_WIDTH), F32),
        jax.ShapeDtypeStruct((bsz, L, RET_WIDTH), BF16),
        jax.ShapeDtypeStruct((bsz, L, RET_WIDTH), F32),
        jax.ShapeDtypeStruct((L, bsz * S5_WIDTH), F32),
    ]
    out_specs = [row3(SB_WIDTH)] * 7 + [pl.BlockSpec((tm, S5_WIDTH), lambda b, t: (t, b))]
    return pl.pallas_call(
        _in_kernel,
        grid=grid,
        in_specs=[
            row3(D_MODEL),
            const((1, D_MODEL)),
            const((D_MODEL, D_IN)),
            const((4, SB_WIDTH)),
            const((SB_WIDTH, SB_WIDTH)),
            pl.BlockSpec((tm, LANES), lambda b, t: (t, 0)),
            pl.BlockSpec((tm, LANES), lambda b, t: (t, 0)),
        ],
        out_specs=out_specs,
        out_shape=out_shapes,
        compiler_params=pltpu.CompilerParams(
            dimension_semantics=("arbitrary", "arbitrary"), vmem_limit_bytes=VMEM_LIMIT_BYTES),
        name="in_proj",
    )(h, g1, w_in, head_gains, bd, cos_t, sin_t)


def _sb_kernel(q_ref, k_ref, v_ref, tri_ref, o_ref, acc_ref, c_ref, *, tq, tk):
    qi = pl.program_id(2)
    q = q_ref[0]
    first = _first_head_lanes()
    acc_ref[...] = jnp.zeros_like(acc_ref)
    c_ref[...] = jnp.zeros_like(c_ref)
    diag_blocks = tq // tk

    def tile(j, masked):
        start = pl.multiple_of(j * tk, tk)
        kk = _head_split(k_ref[0, pl.ds(start, tk), :], first)
        vv = _head_split(v_ref[0, pl.ds(start, tk), :], first)
        z = _dot_nt(q, kk)
        nz = -z
        log1m = jnp.minimum(nz, 0.0) - jnp.log(1.0 + jnp.exp(jnp.minimum(z, nz)))
        log_beta = z + log1m
        if masked:
            t_idx = qi * tq + lax.broadcasted_iota(jnp.int32, (tq, 1), 0)
            s_idx = j * tk + (lax.broadcasted_iota(jnp.int32, (1, 2 * tk), 1) & (tk - 1))
            mask = (s_idx < t_idx) & (s_idx >= PAD_FRONT)
            log1m = jnp.where(mask, log1m, 0.0)
        between = _split_dot(log1m, tri_ref[...]) + c_ref[...]
        w = jnp.exp(log_beta + between)
        if masked:
            w = jnp.where(mask, w, 0.0)
        acc_ref[...] += _dot(w.astype(BF16), vv)
        rs0 = jnp.sum(log1m[:, :tk], axis=-1, keepdims=True)
        rs1 = jnp.sum(log1m[:, tk:], axis=-1, keepdims=True)
        c_ref[...] += jnp.concatenate(
            [jnp.broadcast_to(rs0, (tq, tk)), jnp.broadcast_to(rs1, (tq, tk))], axis=1)

    for d in range(diag_blocks - 1, -1, -1):
        tile(qi * diag_blocks + d, True)

    n_full = qi * diag_blocks - 1

    def body(i, carry):
        tile(n_full - i, False)
        return carry

    lax.fori_loop(0, n_full, body, 0)

    @pl.when(qi > 0)
    def _():
        tile(0, True)

    o_ref[0] = acc_ref[...].astype(BF16)


def _sb_attention(q, k, v, tri, tq, tk):
    bsz, L, _ = q.shape
    grid = (bsz, HEAD_PAIRS, L // tq)
    return pl.pallas_call(
        functools.partial(_sb_kernel, tq=tq, tk=tk),
        grid=grid,
        in_specs=[
            pl.BlockSpec((1, tq, LANES), lambda b, p, i: (b, i, p)),
            pl.BlockSpec((1, L, LANES), lambda b, p, i: (b, 0, p)),
            pl.BlockSpec((1, L, LANES), lambda b, p, i: (b, 0, p)),
            pl.BlockSpec((2 * tk, 2 * tk), lambda b, p, i: (0, 0)),
        ],
        out_specs=pl.BlockSpec((1, tq, LANES), lambda b, p, i: (b, i, p)),
        out_shape=jax.ShapeDtypeStruct((bsz, L, SB_WIDTH), BF16),
        scratch_shapes=[pltpu.VMEM((tq, LANES), F32), pltpu.VMEM((tq, 2 * tk), F32)],
        compiler_params=pltpu.CompilerParams(
            dimension_semantics=("arbitrary", "arbitrary", "arbitrary"), vmem_limit_bytes=VMEM_LIMIT_BYTES),
        name="sb_attention",
    )(q, k, v, tri)


def _ret_kernel(q_ref, k_ref, v_ref, g_ref, dq_ref, dk_ref, din_ref, cd_ref, go_ref, bd_ref,
                o_ref, s_ref):
    n = pl.program_id(2)

    @pl.when(n == 0)
    def _():
        s_ref[...] = jnp.zeros_like(s_ref)

    first = _first_head_lanes()
    q = q_ref[0]
    k = k_ref[0] * (HEAD_DIM ** -0.5)
    v = v_ref[0]
    g = g_ref[0]
    scores = _dot_nt(q.astype(BF16), _head_split(k.astype(BF16), first)) * din_ref[0]
    inner = _dot(scores.astype(BF16), _head_split(v, first))
    state = s_ref[...]
    cross = _dot((q * dq_ref[0]).astype(BF16), state.astype(BF16))
    upd = _dot_tn((k * dk_ref[0]).astype(BF16), v)
    row_head = lax.broadcasted_iota(jnp.int32, (LANES, 1), 0) < HEAD_DIM
    same_head = row_head == first
    s_ref[...] = state * cd_ref[0] + jnp.where(same_head, upd, 0.0)
    o = inner + cross
    ms = _split_dot(o * o, bd_ref[...])
    on = o * lax.rsqrt(ms + RMS_EPS) * go_ref[...]
    o_ref[0] = (on * (g * jax.nn.sigmoid(g))).astype(BF16)


def _retention(q, k, v, g, dq, dk, din, cd, gout, bd):
    bsz, L, _ = q.shape
    grid = (bsz, HEAD_PAIRS, L // CHUNK)
    blk = pl.BlockSpec((1, CHUNK, LANES), lambda b, p, n: (b, n, p))
    per_pair = lambda shape: pl.BlockSpec((1,) + shape, lambda b, p, n: (p, 0, 0))
    return pl.pallas_call(
        _ret_kernel,
        grid=grid,
        in_specs=[
            blk, blk, blk, blk,
            per_pair((CHUNK, LANES)), per_pair((CHUNK, LANES)), per_pair((CHUNK, 2 * CHUNK)),
            per_pair((1, LANES)),
            pl.BlockSpec((1, LANES), lambda b, p, n: (0, p)),
            pl.BlockSpec((LANES, LANES), lambda b, p, n: (0, 0)),
        ],
        out_specs=blk,
        out_shape=jax.ShapeDtypeStruct((bsz, L, RET_WIDTH), BF16),
        scratch_shapes=[pltpu.VMEM((LANES, LANES), F32)],
        compiler_params=pltpu.CompilerParams(
            dimension_semantics=("arbitrary", "arbitrary", "arbitrary")),
        name="retention",
    )(q, k, v, g, dq, dk, din, cd, gout, bd)


def _s5_kernel(u_ref, bt_ref, ar_ref, ai_ref, ct_ref, d_ref, wg_ref, o_ref,
               bu_ref, x_ref, st_ref, *, steps, bsz, slab):
    @pl.when(pl.program_id(0) == 0)
    def _():
        st_ref[...] = jnp.zeros_like(st_ref)

    u = u_ref[...]
    bu_ref[...] = _dot(u.astype(BF16), bt_ref[...])

    for ls in range(S5_NSTATE // slab):
        re = slice(ls * slab, (ls + 1) * slab)
        im = slice(S5_NSTATE + ls * slab, S5_NSTATE + (ls + 1) * slab)
        ar = jnp.broadcast_to(ar_ref[:, re], (bsz, slab))
        ai = jnp.broadcast_to(ai_ref[:, re], (bsz, slab))

        def body(t, carry):
            xr, xi = carry
            rows = pl.ds(pl.multiple_of(t * bsz, bsz), bsz)
            nr = ar * xr - ai * xi + bu_ref[rows, re]
            ni = ar * xi + ai * xr + bu_ref[rows, im]
            x_ref[rows, re] = nr.astype(BF16)
            x_ref[rows, im] = ni.astype(BF16)
            return nr, ni

        xr, xi = lax.fori_loop(0, steps, body, (st_ref[:, re], st_ref[:, im]), unroll=8)
        st_ref[:, re] = xr
        st_ref[:, im] = xi

    y = _dot(x_ref[...], ct_ref[...]) + d_ref[...] * u
    y = jax.nn.gelu(y)
    o_ref[...] = (y * jax.nn.sigmoid(_dot(y.astype(BF16), wg_ref[...]))).astype(BF16)


def _s5(u_tm, bt, ar, ai, ct, d, wg, bsz, steps):
    rows = u_tm.shape[0]
    blk_rows = steps * bsz
    const = lambda shape: pl.BlockSpec(shape, lambda t: (0, 0))
    return pl.pallas_call(
        functools.partial(_s5_kernel, steps=steps, bsz=bsz, slab=512),
        grid=(rows // blk_rows,),
        in_specs=[
            pl.BlockSpec((blk_rows, S5_WIDTH), lambda t: (t, 0)),
            const((S5_WIDTH, 2 * S5_NSTATE)),
            const((1, S5_NSTATE)), const((1, S5_NSTATE)),
            const((2 * S5_NSTATE, S5_WIDTH)),
            const((1, S5_WIDTH)),
            const((S5_WIDTH, S5_WIDTH)),
        ],
        out_specs=pl.BlockSpec((blk_rows, S5_WIDTH), lambda t: (t, 0)),
        out_shape=jax.ShapeDtypeStruct((rows, S5_WIDTH), BF16),
        scratch_shapes=[
            pltpu.VMEM((blk_rows, 2 * S5_NSTATE), F32),
            pltpu.VMEM((blk_rows, 2 * S5_NSTATE), BF16),
            pltpu.VMEM((bsz, 2 * S5_NSTATE), F32),
        ],
        compiler_params=pltpu.CompilerParams(
            dimension_semantics=("arbitrary",), vmem_limit_bytes=VMEM_LIMIT_BYTES),
        name="s5_mixer",
    )(u_tm, bt, ar, ai, ct, d, wg)


FF_TILE_STARTS = (0, 768, 1536, 2304, D_FF)


def _out_ffn_kernel(h_ref, sb_ref, ro_ref, so_ref, wo_ref, g2_ref, wg_ref, wu_ref, wd_ref, o_ref, *, tm):
    mix = (_dot(sb_ref[0], wo_ref[0:SB_WIDTH, :])
           + _dot(ro_ref[0], wo_ref[SB_WIDTH:SB_WIDTH + RET_WIDTH, :])
           + _dot(so_ref[...], wo_ref[SB_WIDTH + RET_WIDTH:, :]))
    h1 = h_ref[0] + mix
    ms = jnp.mean(h1 * h1, axis=-1, keepdims=True)
    hn = (h1 * lax.rsqrt(ms + RMS_EPS) * g2_ref[...]).astype(BF16)
    ffn = jnp.zeros_like(h1)
    for lo, hi in zip(FF_TILE_STARTS[:-1], FF_TILE_STARTS[1:]):
        gate = _dot(hn, wg_ref[:, lo:hi])
        up = _dot(hn, wu_ref[:, lo:hi])
        act = (gate * jax.nn.sigmoid(gate)) * up
        ffn = ffn + _dot(act.astype(BF16), wd_ref[lo:hi, :])
    h2 = h1 + ffn
    pos = pl.program_id(1) * tm + lax.broadcasted_iota(jnp.int32, (tm, 1), 0)
    o_ref[0] = jnp.where(pos >= PAD_FRONT, h2, 0.0)


def _out_ffn(h, sb_o, ro, so_tm, w_out, g2, w_gate, w_up, w_down, tm):
    bsz, L, _ = h.shape
    grid = (bsz, L // tm)
    row3 = lambda w: pl.BlockSpec((1, tm, w), lambda b, t: (b, t, 0))
    const = lambda shape: pl.BlockSpec(shape, lambda b, t: (0, 0), pipeline_mode=pl.Buffered(1))
    return pl.pallas_call(
        functools.partial(_out_ffn_kernel, tm=tm),
        grid=grid,
        in_specs=[
            row3(D_MODEL), row3(SB_WIDTH), row3(RET_WIDTH),
            pl.BlockSpec((tm, S5_WIDTH), lambda b, t: (t, b)),
            const((D_MODEL, D_MODEL)), const((1, D_MODEL)),
            const((D_MODEL, D_FF)), const((D_MODEL, D_FF)), const((D_FF, D_MODEL)),
        ],
        out_specs=row3(D_MODEL),
        out_shape=jax.ShapeDtypeStruct((bsz, L, D_MODEL), F32),
        compiler_params=pltpu.CompilerParams(
            dimension_semantics=("arbitrary", "arbitrary"), vmem_limit_bytes=VMEM_LIMIT_BYTES),
        name="out_ffn",
    )(h, sb_o, ro, so_tm, w_out, g2, w_gate, w_up, w_down)


def _block_diag_mean(width):
    idx = jnp.arange(width) // HEAD_DIM
    return jnp.where(idx[:, None] == idx[None, :], 1.0 / HEAD_DIM, 0.0).astype(BF16)


def _suffix_count_matrix(tk):
    idx = jnp.arange(2 * tk)
    same = (idx[:, None] // tk) == (idx[None, :] // tk)
    return jnp.where(same & (idx[:, None] > idx[None, :]), 1.0, 0.0).astype(BF16)


def _rope_tables(L):
    half = HEAD_DIM // 2
    pos = (jnp.arange(L) - PAD_FRONT).astype(F32)
    inv = ROPE_BASE ** (-jnp.arange(half, dtype=F32) / half)
    ang = pos[:, None] * inv[None, :]
    cos = jnp.cos(ang)
    sin = jnp.sin(ang)
    cos_t = jnp.tile(jnp.concatenate([cos, cos], axis=1), (1, LANES // HEAD_DIM))
    sin_t = jnp.tile(jnp.concatenate([-sin, sin], axis=1), (1, LANES // HEAD_DIM))
    return cos_t, sin_t


def _retention_tables():
    log_gamma = jnp.log1p(-jnp.exp2(-5.0 - jnp.arange(RET_HEADS, dtype=F32)))
    i = jnp.arange(CHUNK, dtype=F32)
    diff = i[:, None] - i[None, :]
    decay_in = jnp.where(diff >= 0, jnp.exp(log_gamma[:, None, None] * jnp.maximum(diff, 0.0)), 0.0)
    q_decay = jnp.exp(log_gamma[:, None] * (i + 1.0))
    k_decay = jnp.exp(log_gamma[:, None] * (CHUNK - 1.0 - i))
    chunk_decay = jnp.exp(log_gamma * CHUNK)
    per_lane = lambda t: jnp.repeat(t.T.reshape(CHUNK, HEAD_PAIRS, 2), HEAD_DIM, axis=2).transpose(1, 0, 2)
    dq = per_lane(q_decay)
    dk = per_lane(k_decay)
    din = decay_in.reshape(HEAD_PAIRS, 2, CHUNK, CHUNK).transpose(0, 2, 1, 3).reshape(HEAD_PAIRS, CHUNK, 2 * CHUNK)
    cd = jnp.repeat(chunk_decay.reshape(HEAD_PAIRS, 1, 2), HEAD_DIM, axis=2)
    return dq, dk, din, cd


def _s5_params(lam_re, lam_im, log_dt, b_re, b_im, c_re, c_im):
    dt = jnp.exp(log_dt)[:, None]
    mag = jnp.exp(lam_re * dt)
    ar = mag * jnp.cos(lam_im * dt)
    ai = mag * jnp.sin(lam_im * dt)
    den = lam_re * lam_re + lam_im * lam_im
    fr = ((ar - 1.0) * lam_re + ai * lam_im) / den
    fi = (ai * lam_re - (ar - 1.0) * lam_im) / den
    bbr = fr[..., None] * b_re - fi[..., None] * b_im
    bbi = fr[..., None] * b_im + fi[..., None] * b_re
    eye = jnp.eye(S5_GROUPS, dtype=F32)
    expand_b = lambda m: jnp.einsum('gpc,gh->gchp', m, eye).reshape(S5_WIDTH, S5_NSTATE)
    bt = jnp.concatenate([expand_b(bbr), expand_b(bbi)], axis=1).astype(BF16)
    expand_c = lambda m: jnp.einsum('gcp,gh->hpgc', m, eye).reshape(S5_NSTATE, S5_WIDTH)
    ct = jnp.concatenate([expand_c(c_re), -expand_c(c_im)], axis=0).astype(BF16)
    return bt, ar.reshape(1, S5_NSTATE), ai.reshape(1, S5_NSTATE), ct


def _largest_tile(L, candidates):
    for c in candidates:
        if L % c == 0:
            return c
    raise ValueError(f"no supported tile divides sequence length {L}")


def kernel(x, meta_tokens, norm1_g, w_in, sb_q_g, sb_k_g, ret_q_g, ret_k_g, ret_out_g, s5_lam_re, s5_lam_im,
           s5_log_dt, s5_b_re, s5_b_im, s5_c_re, s5_c_im, s5_d, s5_w_glu, w_out, norm2_g, w_gate, w_up, w_down):
    bsz, seq, _ = x.shape
    depth = w_in.shape[0]
    L = PAD_FRONT + N_META + seq
    assert L % CHUNK == 0
    tm = _largest_tile(L, (528, 384, 128))
    tq = _largest_tile(L, (384, 128))
    s5_steps = _largest_tile(L, (32, 16, 8))

    pad = jnp.zeros((bsz, PAD_FRONT, D_MODEL), x.dtype)
    meta = jnp.broadcast_to(meta_tokens[None].astype(x.dtype), (bsz, N_META, D_MODEL))
    h = jnp.concatenate([pad, meta, x], axis=1)

    bd384 = _block_diag_mean(SB_WIDTH)
    bd128 = _block_diag_mean(LANES)
    tri = _suffix_count_matrix(CHUNK)
    cos_t, sin_t = _rope_tables(L)
    dq, dk, din, cd = _retention_tables()
    tile6 = lambda g: jnp.tile(g, SB_HEADS)

    for l in range(depth):
        head_gains = jnp.stack([tile6(sb_q_g[l]), tile6(sb_k_g[l]), tile6(ret_q_g[l]), tile6(ret_k_g[l])])
        sq, sk, sv, rq, rk, rv, rg, u = _in_proj(
            h, norm1_g[l][None], w_in[l].astype(BF16), head_gains, bd384, cos_t, sin_t, tm)
        sb_o = _sb_attention(sq, sk, sv, tri, tq, CHUNK)
        ro = _retention(rq, rk, rv, rg, dq, dk, din, cd, ret_out_g[l][None], bd128)
        bt, ar, ai, ct = _s5_params(s5_lam_re[l], s5_lam_im[l], s5_log_dt[l], s5_b_re[l], s5_b_im[l],
                                    s5_c_re[l], s5_c_im[l])
        so = _s5(u.reshape(L * bsz, S5_WIDTH), bt, ar, ai, ct, s5_d[l][None], s5_w_glu[l].astype(BF16),
                 bsz, s5_steps)
        h = _out_ffn(h, sb_o, ro, so.reshape(L, bsz * S5_WIDTH), w_out[l].astype(BF16), norm2_g[l][None],
                     w_gate[l].astype(BF16), w_up[l].astype(BF16), w_down[l].astype(BF16), tm)
    return h[:, PAD_FRONT + N_META:]
```

```python
import functools
import math

import jax
import jax.numpy as jnp
from jax import lax
from jax.experimental import pallas as pl
from jax.experimental.pallas import tpu as pltpu

D_MODEL = 1024
HEAD_DIM = 64
SB_HEADS = 6
RET_HEADS = 6
SB_WIDTH = SB_HEADS * HEAD_DIM
RET_WIDTH = RET_HEADS * HEAD_DIM
S5_WIDTH = D_MODEL - SB_WIDTH - RET_WIDTH
S5_GROUP_CH = 16
S5_GROUPS = S5_WIDTH // S5_GROUP_CH
S5_STATE = 64
S5_NSTATE = S5_GROUPS * S5_STATE
D_IN = 3 * SB_WIDTH + 4 * RET_WIDTH + S5_WIDTH
D_FF = 2816
CHUNK = 128
N_META = 16
PAD_FRONT = CHUNK - N_META
ROPE_BASE = 10000.0
RMS_EPS = 1e-6
SB_QUERY_SCALE = HEAD_DIM ** -0.5 * math.log2(math.e)
MASKED_LOG2_WEIGHT = -1e30

LANES = 128
HEAD_PAIRS = SB_HEADS // 2
VMEM_LIMIT_BYTES = 56 * 1024 * 1024

F32 = jnp.float32
BF16 = jnp.bfloat16


def _dot(a, b):
    return jnp.dot(a, b, preferred_element_type=F32)


def _dot_nt(a, b):
    return lax.dot_general(a, b, (((1,), (1,)), ((), ())), preferred_element_type=F32)


def _dot_tn(a, b):
    return lax.dot_general(a, b, (((0,), (0,)), ((), ())), preferred_element_type=F32)


def _split_dot(x, m):
    hi = x.astype(BF16)
    lo = (x - hi.astype(F32)).astype(BF16)
    return _dot(hi, m) + _dot(lo, m)


def _head_split(t, first_head):
    zero = jnp.zeros_like(t)
    return jnp.concatenate([jnp.where(first_head, t, zero), jnp.where(first_head, zero, t)], axis=0)


def _first_head_lanes():
    return lax.broadcasted_iota(jnp.int32, (1, LANES), 1) < HEAD_DIM


def _in_kernel(h_ref, g1_ref, w_ref, hg_ref, bd_ref, cos_ref, sin_ref,
               sq_ref, sk_ref, sv_ref, rq_ref, rk_ref, rv_ref, rg_ref, u_ref):
    x = h_ref[0]
    ms = jnp.mean(x * x, axis=-1, keepdims=True)
    hn = (x * lax.rsqrt(ms + RMS_EPS) * g1_ref[...]).astype(BF16)

    def proj(seg, width=SB_WIDTH):
        return _dot(hn, w_ref[:, seg * SB_WIDTH:seg * SB_WIDTH + width])

    def head_norm(p, gi):
        m = _split_dot(p * p, bd_ref[...])
        return p * lax.rsqrt(m + RMS_EPS) * hg_ref[gi:gi + 1, :]

    lane = lax.broadcasted_iota(jnp.int32, (1, LANES), 1)
    lower = (lane & (HEAD_DIM - 1)) < HEAD_DIM // 2
    cos = cos_ref[...]
    sin = sin_ref[...]

    def rotary(t):
        outs = []
        for c in range(HEAD_PAIRS):
            tc = t[:, c * LANES:(c + 1) * LANES]
            partner = jnp.where(lower, pltpu.roll(tc, LANES - HEAD_DIM // 2, 1),
                                pltpu.roll(tc, HEAD_DIM // 2, 1))
            outs.append(tc * cos + partner * sin)
        return jnp.concatenate(outs, axis=1)

    sq_ref[0] = (head_norm(proj(0), 0) * SB_QUERY_SCALE).astype(BF16)
    sk_ref[0] = head_norm(proj(1), 1).astype(BF16)
    sv_ref[0] = proj(2).astype(BF16)
    rq_ref[0] = rotary(head_norm(proj(3), 2))
    rk_ref[0] = rotary(head_norm(proj(4), 3))
    rv_ref[0] = proj(5).astype(BF16)
    rg_ref[0] = proj(6)
    u_ref[...] = proj(7, S5_WIDTH)


def _in_proj(h, g1, w_in, head_gains, bd, cos_t, sin_t, tm):
    bsz, L, _ = h.shape
    grid = (bsz, L // tm)
    row3 = lambda w: pl.BlockSpec((1, tm, w), lambda b, t: (b, t, 0))
    const = lambda shape: pl.BlockSpec(shape, lambda b, t: (0,) * len(shape), pipeline_mode=pl.Buffered(1))
    out_shapes = [
        jax.ShapeDtypeStruct((bsz, L, SB_WIDTH), BF16),
        jax.ShapeDtypeStruct((bsz, L, SB_WIDTH), BF16),
        jax.ShapeDtypeStruct((bsz, L, SB_WIDTH), BF16),
        jax.ShapeDtypeStruct((bsz, L, RET_WIDTH), F32),
        jax.ShapeDtypeStruct((bsz, L, RET_WIDTH), F32),
        jax.ShapeDtypeStruct((bsz, L, RET_WIDTH), BF16),
        jax.ShapeDtypeStruct((bsz, L, RET_WIDTH), F32),
        jax.ShapeDtypeStruct((L, bsz * S5_WIDTH), F32),
    ]
    out_specs = [row3(SB_WIDTH)] * 7 + [pl.BlockSpec((tm, S5_WIDTH), lambda b, t: (t, b))]
    return pl.pallas_call(
        _in_kernel,
        grid=grid,
        in_specs=[
            row3(D_MODEL),
            const((1, D_MODEL)),
            const((D_MODEL, D_IN)),
            const((4, SB_WIDTH)),
            const((SB_WIDTH, SB_WIDTH)),
            pl.BlockSpec((tm, LANES), lambda b, t: (t, 0)),
            pl.BlockSpec((tm, LANES), lambda b, t: (t, 0)),
        ],
        out_specs=out_specs,
        out_shape=out_shapes,
        compiler_params=pltpu.CompilerParams(
            dimension_semantics=("arbitrary", "arbitrary"), vmem_limit_bytes=VMEM_LIMIT_BYTES),
        name="in_proj",
    )(h, g1, w_in, head_gains, bd, cos_t, sin_t)


def _sb_kernel(q_ref, k_ref, v_ref, tri_ref, o_ref, acc_ref, c_ref, sp_ref, lb_ref, *, tq, tk):
    qi = pl.program_id(2)
    q = q_ref[0]
    first = _first_head_lanes()
    acc_ref[...] = jnp.zeros_like(acc_ref)
    c_ref[...] = jnp.zeros_like(c_ref)
    n_key_blocks = k_ref.shape[1] // tk
    width = 2 * tk
    t_idx = qi * tq + lax.broadcasted_iota(jnp.int32, (tq, 1), 0)
    s_lane = lax.broadcasted_iota(jnp.int32, (1, width), 1) & (tk - 1)

    n_pairs = ((qi + 1) * (tq // tk) + 1) // 2

    def block_rows(j):
        return pl.ds(pl.multiple_of(jnp.minimum(j, n_key_blocks - 1) * tk, tk), tk)

    def score_dots(m):
        return [_dot_nt(q, _head_split(k_ref[0, block_rows(j), :], first)) for j in (2 * m + 1, 2 * m)]

    def score_logs(zs, m, causal, padded):
        for half, (z, j) in enumerate(zip(zs, (2 * m + 1, 2 * m))):
            neg_abs = lax.bitcast_convert_type(
                lax.bitcast_convert_type(z, jnp.uint32) | jnp.uint32(0x80000000), F32)
            sp = jnp.maximum(z, 0.0) + jnp.log2(1.0 + jnp.exp2(neg_abs))
            lb = z - sp
            if causal or padded:
                s_idx = j * tk + s_lane
                mask = s_idx >= PAD_FRONT
                if causal:
                    mask = mask & (s_idx < t_idx)
                sp = jnp.where(mask, sp, 0.0)
                lb = jnp.where(mask, lb, MASKED_LOG2_WEIGHT)
            cols = slice(half * width, (half + 1) * width)
            sp_ref[:, cols] = sp
            lb_ref[:, cols] = lb

    def weights():
        c = c_ref[...]
        ws = []
        for half in range(2):
            cols = slice(half * width, (half + 1) * width)
            sp = sp_ref[:, cols]
            hi = sp.astype(BF16)
            lo = (sp - hi.astype(F32)).astype(BF16)
            between = _dot(jnp.concatenate([hi, lo], axis=1), tri_ref[...]) + c
            ws.append(jnp.exp2(lb_ref[:, cols] - between).astype(BF16))
            rs0 = jnp.sum(sp[:, :tk], axis=-1, keepdims=True)
            rs1 = jnp.sum(sp[:, tk:], axis=-1, keepdims=True)
            c = c + jnp.concatenate(
                [jnp.broadcast_to(rs0, (tq, tk)), jnp.broadcast_to(rs1, (tq, tk))], axis=1)
        c_ref[...] = c
        return jnp.concatenate(ws, axis=1)

    def accumulate(w, m):
        vv = jnp.concatenate(
            [_head_split(v_ref[0, block_rows(j), :], first) for j in (2 * m + 1, 2 * m)], axis=0)
        acc_ref[...] += _dot(w, vv)

    def step(m, causal, padded):
        w = weights()
        zs = score_dots(m)
        accumulate(w, m + 1)
        score_logs(zs, m, causal, padded)

    score_logs(score_dots(n_pairs - 1), n_pairs - 1, True, True)
    step(n_pairs - 2, True, True)

    def body(i, carry):
        step(n_pairs - 3 - i, False, False)
        return carry

    lax.fori_loop(0, n_pairs - 3, body, 0)

    @pl.when(n_pairs >= 3)
    def _():
        step(0, False, True)

    accumulate(weights(), 0)
    o_ref[0] = acc_ref[...].astype(BF16)


def _sb_attention(q, k, v, tri, tq, tk):
    bsz, L, _ = q.shape
    assert tq % tk == 0 and tq // tk >= 3, "the pipeline prologue needs two key-block pairs per query block"
    grid = (bsz, HEAD_PAIRS, L // tq)
    return pl.pallas_call(
        functools.partial(_sb_kernel, tq=tq, tk=tk),
        grid=grid,
        in_specs=[
            pl.BlockSpec((1, tq, LANES), lambda b, p, i: (b, i, p)),
            pl.BlockSpec((1, L, LANES), lambda b, p, i: (b, 0, p)),
            pl.BlockSpec((1, L, LANES), lambda b, p, i: (b, 0, p)),
            pl.BlockSpec((4 * tk, 2 * tk), lambda b, p, i: (0, 0)),
        ],
        out_specs=pl.BlockSpec((1, tq, LANES), lambda b, p, i: (b, i, p)),
        out_shape=jax.ShapeDtypeStruct((bsz, L, SB_WIDTH), BF16),
        scratch_shapes=[pltpu.VMEM((tq, LANES), F32), pltpu.VMEM((tq, 2 * tk), F32),
                        pltpu.VMEM((tq, 4 * tk), F32), pltpu.VMEM((tq, 4 * tk), F32)],
        compiler_params=pltpu.CompilerParams(
            dimension_semantics=("arbitrary", "arbitrary", "arbitrary"), vmem_limit_bytes=VMEM_LIMIT_BYTES),
        name="sb_attention",
    )(q, k, v, tri)


def _ret_kernel(q_ref, k_ref, v_ref, g_ref, dq_ref, dk_ref, din_ref, cd_ref, go_ref, bd_ref,
                o_ref, s_ref):
    n = pl.program_id(2)

    @pl.when(n == 0)
    def _():
        s_ref[...] = jnp.zeros_like(s_ref)

    first = _first_head_lanes()
    q = q_ref[0]
    k = k_ref[0] * (HEAD_DIM ** -0.5)
    v = v_ref[0]
    g = g_ref[0]
    scores = _dot_nt(q.astype(BF16), _head_split(k.astype(BF16), first)) * din_ref[0]
    inner = _dot(scores.astype(BF16), _head_split(v, first))
    state = s_ref[...]
    cross = _dot((q * dq_ref[0]).astype(BF16), state.astype(BF16))
    upd = _dot_tn((k * dk_ref[0]).astype(BF16), v)
    row_head = lax.broadcasted_iota(jnp.int32, (LANES, 1), 0) < HEAD_DIM
    same_head = row_head == first
    s_ref[...] = state * cd_ref[0] + jnp.where(same_head, upd, 0.0)
    o = inner + cross
    ms = _split_dot(o * o, bd_ref[...])
    on = o * lax.rsqrt(ms + RMS_EPS) * go_ref[...]
    o_ref[0] = (on * (g * jax.nn.sigmoid(g))).astype(BF16)


def _retention(q, k, v, g, dq, dk, din, cd, gout, bd):
    bsz, L, _ = q.shape
    grid = (bsz, HEAD_PAIRS, L // CHUNK)
    blk = pl.BlockSpec((1, CHUNK, LANES), lambda b, p, n: (b, n, p))
    per_pair = lambda shape: pl.BlockSpec((1,) + shape, lambda b, p, n: (p, 0, 0))
    return pl.pallas_call(
        _ret_kernel,
        grid=grid,
        in_specs=[
            blk, blk, blk, blk,
            per_pair((CHUNK, LANES)), per_pair((CHUNK, LANES)), per_pair((CHUNK, 2 * CHUNK)),
            per_pair((1, LANES)),
            pl.BlockSpec((1, LANES), lambda b, p, n: (0, p)),
            pl.BlockSpec((LANES, LANES), lambda b, p, n: (0, 0)),
        ],
        out_specs=blk,
        out_shape=jax.ShapeDtypeStruct((bsz, L, RET_WIDTH), BF16),
        scratch_shapes=[pltpu.VMEM((LANES, LANES), F32)],
        compiler_params=pltpu.CompilerParams(
            dimension_semantics=("arbitrary", "arbitrary", "arbitrary")),
        name="retention",
    )(q, k, v, g, dq, dk, din, cd, gout, bd)


def _s5_kernel(u_ref, bt_ref, ar_ref, ai_ref, ct_ref, d_ref, wg_ref, o_ref,
               bu_ref, x_ref, st_ref, *, steps, bsz, slab):
    @pl.when(pl.program_id(0) == 0)
    def _():
        st_ref[...] = jnp.zeros_like(st_ref)

    u = u_ref[...]
    bu_ref[...] = _dot(u.astype(BF16), bt_ref[...])

    for ls in range(S5_NSTATE // slab):
        re = slice(ls * slab, (ls + 1) * slab)
        im = slice(S5_NSTATE + ls * slab, S5_NSTATE + (ls + 1) * slab)
        ar = jnp.broadcast_to(ar_ref[:, re], (bsz, slab))
        ai = jnp.broadcast_to(ai_ref[:, re], (bsz, slab))

        def body(t, carry):
            xr, xi = carry
            rows = pl.ds(pl.multiple_of(t * bsz, bsz), bsz)
            nr = ar * xr - ai * xi + bu_ref[rows, re]
            ni = ar * xi + ai * xr + bu_ref[rows, im]
            x_ref[rows, re] = nr.astype(BF16)
            x_ref[rows, im] = ni.astype(BF16)
            return nr, ni

        xr, xi = lax.fori_loop(0, steps, body, (st_ref[:, re], st_ref[:, im]), unroll=8)
        st_ref[:, re] = xr
        st_ref[:, im] = xi

    y = _dot(x_ref[...], ct_ref[...]) + d_ref[...] * u
    y = jax.nn.gelu(y)
    o_ref[...] = (y * jax.nn.sigmoid(_dot(y.astype(BF16), wg_ref[...]))).astype(BF16)


def _s5(u_tm, bt, ar, ai, ct, d, wg, bsz, steps):
    rows = u_tm.shape[0]
    blk_rows = steps * bsz
    const = lambda shape: pl.BlockSpec(shape, lambda t: (0, 0))
    return pl.pallas_call(
        functools.partial(_s5_kernel, steps=steps, bsz=bsz, slab=512),
        grid=(rows // blk_rows,),
        in_specs=[
            pl.BlockSpec((blk_rows, S5_WIDTH), lambda t: (t, 0)),
            const((S5_WIDTH, 2 * S5_NSTATE)),
            const((1, S5_NSTATE)), const((1, S5_NSTATE)),
            const((2 * S5_NSTATE, S5_WIDTH)),
            const((1, S5_WIDTH)),
            const((S5_WIDTH, S5_WIDTH)),
        ],
        out_specs=pl.BlockSpec((blk_rows, S5_WIDTH), lambda t: (t, 0)),
        out_shape=jax.ShapeDtypeStruct((rows, S5_WIDTH), BF16),
        scratch_shapes=[
            pltpu.VMEM((blk_rows, 2 * S5_NSTATE), F32),
            pltpu.VMEM((blk_rows, 2 * S5_NSTATE), BF16),
            pltpu.VMEM((bsz, 2 * S5_NSTATE), F32),
        ],
        compiler_params=pltpu.CompilerParams(
            dimension_semantics=("arbitrary",), vmem_limit_bytes=VMEM_LIMIT_BYTES),
        name="s5_mixer",
    )(u_tm, bt, ar, ai, ct, d, wg)


FF_TILE_STARTS = (0, 768, 1536, 2304, D_FF)


def _out_ffn_kernel(h_ref, sb_ref, ro_ref, so_ref, wo_ref, g2_ref, wg_ref, wu_ref, wd_ref, o_ref, *, tm):
    mix = (_dot(sb_ref[0], wo_ref[0:SB_WIDTH, :])
           + _dot(ro_ref[0], wo_ref[SB_WIDTH:SB_WIDTH + RET_WIDTH, :])
           + _dot(so_ref[...], wo_ref[SB_WIDTH + RET_WIDTH:, :]))
    h1 = h_ref[0] + mix
    ms = jnp.mean(h1 * h1, axis=-1, keepdims=True)
    hn = (h1 * lax.rsqrt(ms + RMS_EPS) * g2_ref[...]).astype(BF16)
    ffn = jnp.zeros_like(h1)
    for lo, hi in zip(FF_TILE_STARTS[:-1], FF_TILE_STARTS[1:]):
        gate = _dot(hn, wg_ref[:, lo:hi])
        up = _dot(hn, wu_ref[:, lo:hi])
        act = (gate * jax.nn.sigmoid(gate)) * up
        ffn = ffn + _dot(act.astype(BF16), wd_ref[lo:hi, :])
    h2 = h1 + ffn
    pos = pl.program_id(1) * tm + lax.broadcasted_iota(jnp.int32, (tm, 1), 0)
    o_ref[0] = jnp.where(pos >= PAD_FRONT, h2, 0.0)


def _out_ffn(h, sb_o, ro, so_tm, w_out, g2, w_gate, w_up, w_down, tm):
    bsz, L, _ = h.shape
    grid = (bsz, L // tm)
    row3 = lambda w: pl.BlockSpec((1, tm, w), lambda b, t: (b, t, 0))
    const = lambda shape: pl.BlockSpec(shape, lambda b, t: (0, 0), pipeline_mode=pl.Buffered(1))
    return pl.pallas_call(
        functools.partial(_out_ffn_kernel, tm=tm),
        grid=grid,
        in_specs=[
            row3(D_MODEL), row3(SB_WIDTH), row3(RET_WIDTH),
            pl.BlockSpec((tm, S5_WIDTH), lambda b, t: (t, b)),
            const((D_MODEL, D_MODEL)), const((1, D_MODEL)),
            const((D_MODEL, D_FF)), const((D_MODEL, D_FF)), const((D_FF, D_MODEL)),
        ],
        out_specs=row3(D_MODEL),
        out_shape=jax.ShapeDtypeStruct((bsz, L, D_MODEL), F32),
        compiler_params=pltpu.CompilerParams(
            dimension_semantics=("arbitrary", "arbitrary"), vmem_limit_bytes=VMEM_LIMIT_BYTES),
        name="out_ffn",
    )(h, sb_o, ro, so_tm, w_out, g2, w_gate, w_up, w_down)


def _block_diag_mean(width):
    idx = jnp.arange(width) // HEAD_DIM
    return jnp.where(idx[:, None] == idx[None, :], 1.0 / HEAD_DIM, 0.0).astype(BF16)


def _suffix_count_matrix(tk):
    idx = jnp.arange(2 * tk)
    same = (idx[:, None] // tk) == (idx[None, :] // tk)
    m = jnp.where(same & (idx[:, None] > idx[None, :]), 1.0, 0.0).astype(BF16)
    return jnp.concatenate([m, m], axis=0)


def _rope_tables(L):
    half = HEAD_DIM // 2
    pos = (jnp.arange(L) - PAD_FRONT).astype(F32)
    inv = ROPE_BASE ** (-jnp.arange(half, dtype=F32) / half)
    ang = pos[:, None] * inv[None, :]
    cos = jnp.cos(ang)
    sin = jnp.sin(ang)
    cos_t = jnp.tile(jnp.concatenate([cos, cos], axis=1), (1, LANES // HEAD_DIM))
    sin_t = jnp.tile(jnp.concatenate([-sin, sin], axis=1), (1, LANES // HEAD_DIM))
    return cos_t, sin_t


def _retention_tables():
    log_gamma = jnp.log1p(-jnp.exp2(-5.0 - jnp.arange(RET_HEADS, dtype=F32)))
    i = jnp.arange(CHUNK, dtype=F32)
    diff = i[:, None] - i[None, :]
    decay_in = jnp.where(diff >= 0, jnp.exp(log_gamma[:, None, None] * jnp.maximum(diff, 0.0)), 0.0)
    q_decay = jnp.exp(log_gamma[:, None] * (i + 1.0))
    k_decay = jnp.exp(log_gamma[:, None] * (CHUNK - 1.0 - i))
    chunk_decay = jnp.exp(log_gamma * CHUNK)
    per_lane = lambda t: jnp.repeat(t.T.reshape(CHUNK, HEAD_PAIRS, 2), HEAD_DIM, axis=2).transpose(1, 0, 2)
    dq = per_lane(q_decay)
    dk = per_lane(k_decay)
    din = decay_in.reshape(HEAD_PAIRS, 2, CHUNK, CHUNK).transpose(0, 2, 1, 3).reshape(HEAD_PAIRS, CHUNK, 2 * CHUNK)
    cd = jnp.repeat(chunk_decay.reshape(HEAD_PAIRS, 1, 2), HEAD_DIM, axis=2)
    return dq, dk, din, cd


def _s5_params(lam_re, lam_im, log_dt, b_re, b_im, c_re, c_im):
    dt = jnp.exp(log_dt)[:, None]
    mag = jnp.exp(lam_re * dt)
    ar = mag * jnp.cos(lam_im * dt)
    ai = mag * jnp.sin(lam_im * dt)
    den = lam_re * lam_re + lam_im * lam_im
    fr = ((ar - 1.0) * lam_re + ai * lam_im) / den
    fi = (ai * lam_re - (ar - 1.0) * lam_im) / den
    bbr = fr[..., None] * b_re - fi[..., None] * b_im
    bbi = fr[..., None] * b_im + fi[..., None] * b_re
    eye = jnp.eye(S5_GROUPS, dtype=F32)
    expand_b = lambda m: jnp.einsum('gpc,gh->gchp', m, eye).reshape(S5_WIDTH, S5_NSTATE)
    bt = jnp.concatenate([expand_b(bbr), expand_b(bbi)], axis=1).astype(BF16)
    expand_c = lambda m: jnp.einsum('gcp,gh->hpgc', m, eye).reshape(S5_NSTATE, S5_WIDTH)
    ct = jnp.concatenate([expand_c(c_re), -expand_c(c_im)], axis=0).astype(BF16)
    return bt, ar.reshape(1, S5_NSTATE), ai.reshape(1, S5_NSTATE), ct


def _largest_tile(L, candidates):
    for c in candidates:
        if L % c == 0:
            return c
    raise ValueError(f"no supported tile divides sequence length {L}")


def kernel(x, meta_tokens, norm1_g, w_in, sb_q_g, sb_k_g, ret_q_g, ret_k_g, ret_out_g, s5_lam_re, s5_lam_im,
           s5_log_dt, s5_b_re, s5_b_im, s5_c_re, s5_c_im, s5_d, s5_w_glu, w_out, norm2_g, w_gate, w_up, w_down):
    bsz, seq, _ = x.shape
    depth = w_in.shape[0]
    L = PAD_FRONT + N_META + seq
    assert L % CHUNK == 0
    tm = _largest_tile(L, (528, 384, 128))
    tq = _largest_tile(L, (384,))
    s5_steps = _largest_tile(L, (32, 16, 8))

    pad = jnp.zeros((bsz, PAD_FRONT, D_MODEL), x.dtype)
    meta = jnp.broadcast_to(meta_tokens[None].astype(x.dtype), (bsz, N_META, D_MODEL))
    h = jnp.concatenate([pad, meta, x], axis=1)

    bd384 = _block_diag_mean(SB_WIDTH)
    bd128 = _block_diag_mean(LANES)
    tri = _suffix_count_matrix(CHUNK)
    cos_t, sin_t = _rope_tables(L)
    dq, dk, din, cd = _retention_tables()
    tile6 = lambda g: jnp.tile(g, SB_HEADS)

    for l in range(depth):
        head_gains = jnp.stack([tile6(sb_q_g[l]), tile6(sb_k_g[l]), tile6(ret_q_g[l]), tile6(ret_k_g[l])])
        sq, sk, sv, rq, rk, rv, rg, u = _in_proj(
            h, norm1_g[l][None], w_in[l].astype(BF16), head_gains, bd384, cos_t, sin_t, tm)
        sb_o = _sb_attention(sq, sk, sv, tri, tq, CHUNK)
        ro = _retention(rq, rk, rv, rg, dq, dk, din, cd, ret_out_g[l][None], bd128)
        bt, ar, ai, ct = _s5_params(s5_lam_re[l], s5_lam_im[l], s5_log_dt[l], s5_b_re[l], s5_b_im[l],
                                    s5_c_re[l], s5_c_im[l])
        so = _s5(u.reshape(L * bsz, S5_WIDTH), bt, ar, ai, ct, s5_d[l][None], s5_w_glu[l].astype(BF16),
                 bsz, s5_steps)
        h = _out_ffn(h, sb_o, ro, so.reshape(L, bsz * S5_WIDTH), w_out[l].astype(BF16), norm2_g[l][None],
                     w_gate[l].astype(BF16), w_up[l].astype(BF16), w_down[l].astype(BF16), tm)
    return h[:, PAD_FRONT + N_META:]
```

```python
import functools
import math

import jax
import jax.numpy as jnp
from jax import lax
from jax.experimental import pallas as pl
from jax.experimental.pallas import tpu as pltpu

D_MODEL = 1024
HEAD_DIM = 64
SB_HEADS = 6
RET_HEADS = 6
SB_WIDTH = SB_HEADS * HEAD_DIM
RET_WIDTH = RET_HEADS * HEAD_DIM
S5_WIDTH = D_MODEL - SB_WIDTH - RET_WIDTH
S5_GROUP_CH = 16
S5_GROUPS = S5_WIDTH // S5_GROUP_CH
S5_STATE = 64
S5_NSTATE = S5_GROUPS * S5_STATE
D_IN = 3 * SB_WIDTH + 4 * RET_WIDTH + S5_WIDTH
D_FF = 2816
CHUNK = 128
N_META = 16
PAD_FRONT = CHUNK - N_META
ROPE_BASE = 10000.0
RMS_EPS = 1e-6
SB_QUERY_SCALE = HEAD_DIM ** -0.5 * math.log2(math.e)
MASKED_LOG2_WEIGHT = -1e30

LANES = 128
HEAD_PAIRS = SB_HEADS // 2
VMEM_LIMIT_BYTES = 56 * 1024 * 1024

F32 = jnp.float32
BF16 = jnp.bfloat16


def _dot(a, b):
    return jnp.dot(a, b, preferred_element_type=F32)


def _dot_nt(a, b):
    return lax.dot_general(a, b, (((1,), (1,)), ((), ())), preferred_element_type=F32)


def _dot_tn(a, b):
    return lax.dot_general(a, b, (((0,), (0,)), ((), ())), preferred_element_type=F32)


def _split_dot(x, m):
    hi = x.astype(BF16)
    lo = (x - hi.astype(F32)).astype(BF16)
    return _dot(hi, m) + _dot(lo, m)


def _head_split(t, first_head):
    zero = jnp.zeros_like(t)
    return jnp.concatenate([jnp.where(first_head, t, zero), jnp.where(first_head, zero, t)], axis=0)


def _first_head_lanes():
    return lax.broadcasted_iota(jnp.int32, (1, LANES), 1) < HEAD_DIM


def _in_kernel(h_ref, g1_ref, w_ref, hg_ref, bd_ref, cos_ref, sin_ref,
               sq_ref, sk_ref, sv_ref, rq_ref, rk_ref, rv_ref, rg_ref, u_ref):
    x = h_ref[0]
    ms = jnp.mean(x * x, axis=-1, keepdims=True)
    hn = (x * lax.rsqrt(ms + RMS_EPS) * g1_ref[...]).astype(BF16)

    def proj_pair(seg):
        p = _dot(hn, w_ref[:, seg * SB_WIDTH:min((seg + 2) * SB_WIDTH, D_IN)])
        return p[:, :SB_WIDTH], p[:, SB_WIDTH:]

    def head_norm(p, gi):
        sq = (p * p).astype(BF16)
        m = jnp.concatenate(
            [_dot(sq[:, c * LANES:(c + 1) * LANES], bd_ref[...]) for c in range(HEAD_PAIRS)], axis=1)
        return p * lax.rsqrt(m + RMS_EPS) * hg_ref[gi:gi + 1, :]

    lane = lax.broadcasted_iota(jnp.int32, (1, LANES), 1)
    lower = (lane & (HEAD_DIM - 1)) < HEAD_DIM // 2
    cos = cos_ref[...]
    sin = sin_ref[...]

    def rotary(t):
        outs = []
        for c in range(HEAD_PAIRS):
            tc = t[:, c * LANES:(c + 1) * LANES]
            partner = jnp.where(lower, pltpu.roll(tc, LANES - HEAD_DIM // 2, 1),
                                pltpu.roll(tc, HEAD_DIM // 2, 1))
            outs.append(tc * cos + partner * sin)
        return jnp.concatenate(outs, axis=1)

    sq, sk = proj_pair(0)
    sq_ref[0] = (head_norm(sq, 0) * SB_QUERY_SCALE).astype(BF16)
    sk_ref[0] = head_norm(sk, 1).astype(BF16)
    sv, rq = proj_pair(2)
    sv_ref[0] = sv.astype(BF16)
    rq_ref[0] = rotary(head_norm(rq, 2))
    rk, rv = proj_pair(4)
    rk_ref[0] = rotary(head_norm(rk, 3))
    rv_ref[0] = rv.astype(BF16)
    rg_ref[0], u_ref[...] = proj_pair(6)


def _in_proj(h, g1, w_in, head_gains, bd, cos_t, sin_t, tm):
    bsz, L, _ = h.shape
    grid = (bsz, L // tm)
    row3 = lambda w: pl.BlockSpec((1, tm, w), lambda b, t: (b, t, 0))
    const = lambda shape: pl.BlockSpec(shape, lambda b, t: (0,) * len(shape), pipeline_mode=pl.Buffered(1))
    out_shapes = [
        jax.ShapeDtypeStruct((bsz, L, SB_WIDTH), BF16),
        jax.ShapeDtypeStruct((bsz, L, SB_WIDTH), BF16),
        jax.ShapeDtypeStruct((bsz, L, SB_WIDTH), BF16),
        jax.ShapeDtypeStruct((bsz, L, RET_WIDTH), F32),
        jax.ShapeDtypeStruct((bsz, L, RET_WIDTH), F32),
        jax.ShapeDtypeStruct((bsz, L, RET_WIDTH), BF16),
        jax.ShapeDtypeStruct((bsz, L, RET_WIDTH), F32),
        jax.ShapeDtypeStruct((L, bsz * S5_WIDTH), F32),
    ]
    out_specs = [row3(SB_WIDTH)] * 7 + [pl.BlockSpec((tm, S5_WIDTH), lambda b, t: (t, b))]
    return pl.pallas_call(
        _in_kernel,
        grid=grid,
        in_specs=[
            row3(D_MODEL),
            const((1, D_MODEL)),
            const((D_MODEL, D_IN)),
            const((4, SB_WIDTH)),
            const((LANES, LANES)),
            pl.BlockSpec((tm, LANES), lambda b, t: (t, 0)),
            pl.BlockSpec((tm, LANES), lambda b, t: (t, 0)),
        ],
        out_specs=out_specs,
        out_shape=out_shapes,
        compiler_params=pltpu.CompilerParams(
            dimension_semantics=("arbitrary", "arbitrary"), vmem_limit_bytes=VMEM_LIMIT_BYTES),
        name="in_proj",
    )(h, g1, w_in, head_gains, bd, cos_t, sin_t)


def _sb_kernel(q_ref, k_ref, v_ref, tri_ref, o_ref, acc_ref, c_ref, sp_ref, z_ref, *, tq, tk):
    qi = pl.program_id(1)
    first = _first_head_lanes()
    acc_ref[...] = jnp.zeros_like(acc_ref)
    c_ref[...] = jnp.zeros_like(c_ref)
    n_key_blocks = k_ref.shape[1] // tk
    width = 2 * tk
    t_idx = qi * tq + lax.broadcasted_iota(jnp.int32, (tq, 1), 0)
    s_lane = lax.broadcasted_iota(jnp.int32, (1, width), 1) & (tk - 1)

    n_pairs = ((qi + 1) * (tq // tk) + 1) // 2

    def block_rows(j):
        return pl.ds(pl.multiple_of(jnp.minimum(j, n_key_blocks - 1) * tk, tk), tk)

    def lanes(p):
        return slice(p * LANES, (p + 1) * LANES)

    def score_dots(m, p):
        q = q_ref[0, :, lanes(p)]
        return [_dot_nt(q, _head_split(k_ref[0, block_rows(j), lanes(p)], first)) for j in (2 * m + 1, 2 * m)]

    def score_logs(zs, m, p, slot, causal, padded):
        for half, (z, j) in enumerate(zip(zs, (2 * m + 1, 2 * m))):
            neg_abs = lax.bitcast_convert_type(
                lax.bitcast_convert_type(z, jnp.uint32) | jnp.uint32(0x80000000), F32)
            sp = jnp.maximum(z, 0.0) + jnp.log2(1.0 + jnp.exp2(neg_abs))
            if causal or padded:
                s_idx = j * tk + s_lane
                mask = s_idx >= PAD_FRONT
                if causal:
                    mask = mask & (s_idx < t_idx)
                sp = jnp.where(mask, sp, 0.0)
                z = jnp.where(mask, z, MASKED_LOG2_WEIGHT)
            cols = slice(half * width, (half + 1) * width)
            sp_ref[p, slot, :, cols] = sp.astype(BF16)
            z_ref[p, slot, :, cols] = z

    def weights(slot, p):
        c = c_ref[p]
        ws = []
        for half in range(2):
            cols = slice(half * width, (half + 1) * width)
            later = _dot(sp_ref[p, slot, :, cols], tri_ref[...]) + c
            ws.append(jnp.exp2(z_ref[p, slot, :, cols] - later).astype(BF16))
            c = jnp.concatenate([jnp.broadcast_to(later[:, 0:1], (tq, tk)),
                                 jnp.broadcast_to(later[:, tk:tk + 1], (tq, tk))], axis=1)
        c_ref[p] = c
        return jnp.concatenate(ws, axis=1)

    def accumulate(w, m, p):
        vv = jnp.concatenate(
            [_head_split(v_ref[0, block_rows(j), lanes(p)], first) for j in (2 * m + 1, 2 * m)], axis=0)
        acc_ref[:, lanes(p)] += _dot(w, vv)

    def first_scores(m, slot):
        for p in range(HEAD_PAIRS):
            score_logs(score_dots(m, p), m, p, slot, True, True)

    def step(m, slot, causal, padded):
        for p in range(HEAD_PAIRS):
            score_logs(score_dots(m, p), m, p, slot, causal, padded)
            accumulate(weights(1 - slot, p), m + 1, p)

    for parity in (0, 1):
        @pl.when((n_pairs - 1) % 2 == parity)
        def _():
            first_scores(n_pairs - 1, parity)
            step(n_pairs - 2, 1 - parity, True, True)

    @pl.when(jnp.logical_and(n_pairs >= 4, n_pairs % 2 == 0))
    def _():
        step(n_pairs - 3, 1, False, False)

    n_double = (n_pairs - 3) // 2

    def body(i, carry):
        m = 2 * (n_double - i)
        step(m, 0, False, False)
        step(m - 1, 1, False, False)
        return carry

    lax.fori_loop(0, n_double, body, 0)

    @pl.when(n_pairs >= 3)
    def _():
        step(0, 0, False, True)

    for p in range(HEAD_PAIRS):
        accumulate(weights(0, p), 0, p)
    o_ref[0] = acc_ref[...].astype(BF16)


def _sb_attention(q, k, v, tri, tq, tk):
    bsz, L, _ = q.shape
    assert tq % tk == 0 and tq // tk >= 3, "the pipeline prologue needs two key-block pairs per query block"
    grid = (bsz, L // tq)
    return pl.pallas_call(
        functools.partial(_sb_kernel, tq=tq, tk=tk),
        grid=grid,
        in_specs=[
            pl.BlockSpec((1, tq, SB_WIDTH), lambda b, i: (b, i, 0)),
            pl.BlockSpec((1, L, SB_WIDTH), lambda b, i: (b, 0, 0)),
            pl.BlockSpec((1, L, SB_WIDTH), lambda b, i: (b, 0, 0)),
            pl.BlockSpec((2 * tk, 2 * tk), lambda b, i: (0, 0)),
        ],
        out_specs=pl.BlockSpec((1, tq, SB_WIDTH), lambda b, i: (b, i, 0)),
        out_shape=jax.ShapeDtypeStruct((bsz, L, SB_WIDTH), BF16),
        scratch_shapes=[pltpu.VMEM((tq, SB_WIDTH), F32), pltpu.VMEM((HEAD_PAIRS, tq, 2 * tk), F32),
                        pltpu.VMEM((HEAD_PAIRS, 2, tq, 4 * tk), BF16),
                        pltpu.VMEM((HEAD_PAIRS, 2, tq, 4 * tk), F32)],
        compiler_params=pltpu.CompilerParams(
            dimension_semantics=("arbitrary", "arbitrary"), vmem_limit_bytes=VMEM_LIMIT_BYTES),
        name="sb_attention",
    )(q, k, v, tri)


def _ret_kernel(q_ref, k_ref, v_ref, g_ref, dq_ref, dk_ref, din_ref, cd_ref, go_ref, bd_ref,
                o_ref, s_ref, *, chunks):
    @pl.when(pl.program_id(1) == 0)
    def _():
        s_ref[...] = jnp.zeros_like(s_ref)

    first = _first_head_lanes()
    same_head = (lax.broadcasted_iota(jnp.int32, (LANES, 1), 0) < HEAD_DIM) == first
    for p in range(HEAD_PAIRS):
        lanes = slice(p * LANES, (p + 1) * LANES)
        state = s_ref[p]
        for c in range(chunks):
            rows = slice(c * CHUNK, (c + 1) * CHUNK)
            q = q_ref[0, rows, lanes]
            k = k_ref[0, rows, lanes] * (HEAD_DIM ** -0.5)
            v = v_ref[0, rows, lanes]
            g = g_ref[0, rows, lanes]
            scores = _dot_nt(q.astype(BF16), _head_split(k.astype(BF16), first)) * din_ref[p]
            inner = _dot(scores.astype(BF16), _head_split(v, first))
            cross = _dot((q * dq_ref[p]).astype(BF16), state.astype(BF16))
            upd = _dot_tn((k * dk_ref[p]).astype(BF16), v)
            state = state * cd_ref[p] + jnp.where(same_head, upd, 0.0)
            o = inner + cross
            ms = _split_dot(o * o, bd_ref[...])
            on = o * lax.rsqrt(ms + RMS_EPS) * go_ref[:, lanes]
            o_ref[0, rows, lanes] = (on * (g * jax.nn.sigmoid(g))).astype(BF16)
        s_ref[p] = state


def _retention(q, k, v, g, dq, dk, din, cd, gout, bd, chunks):
    bsz, L, _ = q.shape
    rows = chunks * CHUNK
    grid = (bsz, L // rows)
    blk = pl.BlockSpec((1, rows, RET_WIDTH), lambda b, n: (b, n, 0))
    const = lambda shape: pl.BlockSpec(shape, lambda b, n: (0,) * len(shape))
    return pl.pallas_call(
        functools.partial(_ret_kernel, chunks=chunks),
        grid=grid,
        in_specs=[
            blk, blk, blk, blk,
            const((HEAD_PAIRS, CHUNK, LANES)), const((HEAD_PAIRS, CHUNK, LANES)),
            const((HEAD_PAIRS, CHUNK, 2 * CHUNK)), const((HEAD_PAIRS, 1, LANES)),
            const((1, RET_WIDTH)), const((LANES, LANES)),
        ],
        out_specs=blk,
        out_shape=jax.ShapeDtypeStruct((bsz, L, RET_WIDTH), BF16),
        scratch_shapes=[pltpu.VMEM((HEAD_PAIRS, LANES, LANES), F32)],
        compiler_params=pltpu.CompilerParams(dimension_semantics=("arbitrary", "arbitrary")),
        name="retention",
    )(q, k, v, g, dq, dk, din, cd, gout, bd)


def _s5_kernel(u_ref, bt_ref, ar_ref, ai_ref, ct_ref, d_ref, wg_ref, o_ref,
               bu_ref, x_ref, st_ref, *, steps, bsz, slab):
    @pl.when(pl.program_id(0) == 0)
    def _():
        st_ref[...] = jnp.zeros_like(st_ref)

    u = u_ref[...]
    bu_ref[...] = _dot(u.astype(BF16), bt_ref[...])

    for ls in range(S5_NSTATE // slab):
        re = slice(ls * slab, (ls + 1) * slab)
        im = slice(S5_NSTATE + ls * slab, S5_NSTATE + (ls + 1) * slab)
        ar = jnp.broadcast_to(ar_ref[:, re], (bsz, slab))
        ai = jnp.broadcast_to(ai_ref[:, re], (bsz, slab))

        def body(t, carry):
            xr, xi = carry
            rows = pl.ds(pl.multiple_of(t * bsz, bsz), bsz)
            nr = ar * xr - ai * xi + bu_ref[rows, re]
            ni = ar * xi + ai * xr + bu_ref[rows, im]
            x_ref[rows, re] = nr.astype(BF16)
            x_ref[rows, im] = ni.astype(BF16)
            return nr, ni

        xr, xi = lax.fori_loop(0, steps, body, (st_ref[:, re], st_ref[:, im]), unroll=8)
        st_ref[:, re] = xr
        st_ref[:, im] = xi

    y = _dot(x_ref[...], ct_ref[...]) + d_ref[...] * u
    y = jax.nn.gelu(y)
    o_ref[...] = (y * jax.nn.sigmoid(_dot(y.astype(BF16), wg_ref[...]))).astype(BF16)


def _s5(u_tm, bt, ar, ai, ct, d, wg, bsz, steps):
    rows = u_tm.shape[0]
    blk_rows = steps * bsz
    const = lambda shape: pl.BlockSpec(shape, lambda t: (0, 0))
    return pl.pallas_call(
        functools.partial(_s5_kernel, steps=steps, bsz=bsz, slab=512),
        grid=(rows // blk_rows,),
        in_specs=[
            pl.BlockSpec((blk_rows, S5_WIDTH), lambda t: (t, 0)),
            const((S5_WIDTH, 2 * S5_NSTATE)),
            const((1, S5_NSTATE)), const((1, S5_NSTATE)),
            const((2 * S5_NSTATE, S5_WIDTH)),
            const((1, S5_WIDTH)),
            const((S5_WIDTH, S5_WIDTH)),
        ],
        out_specs=pl.BlockSpec((blk_rows, S5_WIDTH), lambda t: (t, 0)),
        out_shape=jax.ShapeDtypeStruct((rows, S5_WIDTH), BF16),
        scratch_shapes=[
            pltpu.VMEM((blk_rows, 2 * S5_NSTATE), F32),
            pltpu.VMEM((blk_rows, 2 * S5_NSTATE), BF16),
            pltpu.VMEM((bsz, 2 * S5_NSTATE), F32),
        ],
        compiler_params=pltpu.CompilerParams(
            dimension_semantics=("arbitrary",), vmem_limit_bytes=VMEM_LIMIT_BYTES),
        name="s5_mixer",
    )(u_tm, bt, ar, ai, ct, d, wg)


FF_TILE_STARTS = (0, 768, 1536, 2304, D_FF)


def _out_ffn_kernel(h_ref, sb_ref, ro_ref, so_ref, wo_ref, g2_ref, wg_ref, wu_ref, wd_ref, o_ref, *, tm):
    mix = jnp.concatenate([sb_ref[0], ro_ref[0], so_ref[...]], axis=1)
    h1 = h_ref[0] + _dot(mix, wo_ref[...])
    ms = jnp.mean(h1 * h1, axis=-1, keepdims=True)
    hn = (h1 * lax.rsqrt(ms + RMS_EPS) * g2_ref[...]).astype(BF16)
    ffn = jnp.zeros_like(h1)
    for lo, hi in zip(FF_TILE_STARTS[:-1], FF_TILE_STARTS[1:]):
        gate = _dot(hn, wg_ref[:, lo:hi])
        up = _dot(hn, wu_ref[:, lo:hi])
        act = (gate * jax.nn.sigmoid(gate)) * up
        ffn = ffn + _dot(act.astype(BF16), wd_ref[lo:hi, :])
    h2 = h1 + ffn
    pos = pl.program_id(1) * tm + lax.broadcasted_iota(jnp.int32, (tm, 1), 0)
    o_ref[0] = jnp.where(pos >= PAD_FRONT, h2, 0.0)


def _out_ffn(h, sb_o, ro, so_tm, w_out, g2, w_gate, w_up, w_down, tm):
    bsz, L, _ = h.shape
    grid = (bsz, L // tm)
    row3 = lambda w: pl.BlockSpec((1, tm, w), lambda b, t: (b, t, 0))
    const = lambda shape: pl.BlockSpec(shape, lambda b, t: (0, 0), pipeline_mode=pl.Buffered(1))
    return pl.pallas_call(
        functools.partial(_out_ffn_kernel, tm=tm),
        grid=grid,
        in_specs=[
            row3(D_MODEL), row3(SB_WIDTH), row3(RET_WIDTH),
            pl.BlockSpec((tm, S5_WIDTH), lambda b, t: (t, b)),
            const((D_MODEL, D_MODEL)), const((1, D_MODEL)),
            const((D_MODEL, D_FF)), const((D_MODEL, D_FF)), const((D_FF, D_MODEL)),
        ],
        out_specs=row3(D_MODEL),
        out_shape=jax.ShapeDtypeStruct((bsz, L, D_MODEL), F32),
        compiler_params=pltpu.CompilerParams(
            dimension_semantics=("arbitrary", "arbitrary"), vmem_limit_bytes=VMEM_LIMIT_BYTES),
        name="out_ffn",
    )(h, sb_o, ro, so_tm, w_out, g2, w_gate, w_up, w_down)


def _block_diag_mean(width):
    idx = jnp.arange(width) // HEAD_DIM
    return jnp.where(idx[:, None] == idx[None, :], 1.0 / HEAD_DIM, 0.0).astype(BF16)


def _suffix_count_matrix(tk):
    idx = jnp.arange(2 * tk)
    same = (idx[:, None] // tk) == (idx[None, :] // tk)
    return jnp.where(same & (idx[:, None] >= idx[None, :]), 1.0, 0.0).astype(BF16)


def _rope_tables(L):
    half = HEAD_DIM // 2
    pos = (jnp.arange(L) - PAD_FRONT).astype(F32)
    inv = ROPE_BASE ** (-jnp.arange(half, dtype=F32) / half)
    ang = pos[:, None] * inv[None, :]
    cos = jnp.cos(ang)
    sin = jnp.sin(ang)
    cos_t = jnp.tile(jnp.concatenate([cos, cos], axis=1), (1, LANES // HEAD_DIM))
    sin_t = jnp.tile(jnp.concatenate([-sin, sin], axis=1), (1, LANES // HEAD_DIM))
    return cos_t, sin_t


def _retention_tables():
    log_gamma = jnp.log1p(-jnp.exp2(-5.0 - jnp.arange(RET_HEADS, dtype=F32)))
    i = jnp.arange(CHUNK, dtype=F32)
    diff = i[:, None] - i[None, :]
    decay_in = jnp.where(diff >= 0, jnp.exp(log_gamma[:, None, None] * jnp.maximum(diff, 0.0)), 0.0)
    q_decay = jnp.exp(log_gamma[:, None] * (i + 1.0))
    k_decay = jnp.exp(log_gamma[:, None] * (CHUNK - 1.0 - i))
    chunk_decay = jnp.exp(log_gamma * CHUNK)
    per_lane = lambda t: jnp.repeat(t.T.reshape(CHUNK, HEAD_PAIRS, 2), HEAD_DIM, axis=2).transpose(1, 0, 2)
    dq = per_lane(q_decay)
    dk = per_lane(k_decay)
    din = decay_in.reshape(HEAD_PAIRS, 2, CHUNK, CHUNK).transpose(0, 2, 1, 3).reshape(HEAD_PAIRS, CHUNK, 2 * CHUNK)
    cd = jnp.repeat(chunk_decay.reshape(HEAD_PAIRS, 1, 2), HEAD_DIM, axis=2)
    return dq, dk, din, cd


def _s5_params(lam_re, lam_im, log_dt, b_re, b_im, c_re, c_im):
    dt = jnp.exp(log_dt)[:, None]
    mag = jnp.exp(lam_re * dt)
    ar = mag * jnp.cos(lam_im * dt)
    ai = mag * jnp.sin(lam_im * dt)
    den = lam_re * lam_re + lam_im * lam_im
    fr = ((ar - 1.0) * lam_re + ai * lam_im) / den
    fi = (ai * lam_re - (ar - 1.0) * lam_im) / den
    bbr = fr[..., None] * b_re - fi[..., None] * b_im
    bbi = fr[..., None] * b_im + fi[..., None] * b_re
    eye = jnp.eye(S5_GROUPS, dtype=F32)
    expand_b = lambda m: jnp.einsum('gpc,gh->gchp', m, eye).reshape(S5_WIDTH, S5_NSTATE)
    bt = jnp.concatenate([expand_b(bbr), expand_b(bbi)], axis=1).astype(BF16)
    expand_c = lambda m: jnp.einsum('gcp,gh->hpgc', m, eye).reshape(S5_NSTATE, S5_WIDTH)
    ct = jnp.concatenate([expand_c(c_re), -expand_c(c_im)], axis=0).astype(BF16)
    return bt, ar.reshape(1, S5_NSTATE), ai.reshape(1, S5_NSTATE), ct


def _largest_tile(L, candidates):
    for c in candidates:
        if L % c == 0:
            return c
    raise ValueError(f"no supported tile divides sequence length {L}")


def kernel(x, meta_tokens, norm1_g, w_in, sb_q_g, sb_k_g, ret_q_g, ret_k_g, ret_out_g, s5_lam_re, s5_lam_im,
           s5_log_dt, s5_b_re, s5_b_im, s5_c_re, s5_c_im, s5_d, s5_w_glu, w_out, norm2_g, w_gate, w_up, w_down):
    bsz, seq, _ = x.shape
    depth = w_in.shape[0]
    L = PAD_FRONT + N_META + seq
    assert L % CHUNK == 0
    tm = _largest_tile(L, (528, 384, 128))
    tq = _largest_tile(L, (384,))
    s5_steps = _largest_tile(L, (32, 16, 8))

    pad = jnp.zeros((bsz, PAD_FRONT, D_MODEL), x.dtype)
    meta = jnp.broadcast_to(meta_tokens[None].astype(x.dtype), (bsz, N_META, D_MODEL))
    h = jnp.concatenate([pad, meta, x], axis=1)

    bd128 = _block_diag_mean(LANES)
    tri = _suffix_count_matrix(CHUNK)
    cos_t, sin_t = _rope_tables(L)
    dq, dk, din, cd = _retention_tables()
    tile6 = lambda g: jnp.tile(g, SB_HEADS)

    for l in range(depth):
        head_gains = jnp.stack([tile6(sb_q_g[l]), tile6(sb_k_g[l]), tile6(ret_q_g[l]), tile6(ret_k_g[l])])
        sq, sk, sv, rq, rk, rv, rg, u = _in_proj(
            h, norm1_g[l][None], w_in[l].astype(BF16), head_gains, bd128, cos_t, sin_t, tm)
        sb_o = _sb_attention(sq, sk, sv, tri, tq, CHUNK)
        ro = _retention(rq, rk, rv, rg, dq, dk, din, cd, ret_out_g[l][None], bd128, tq // CHUNK)
        bt, ar, ai, ct = _s5_params(s5_lam_re[l], s5_lam_im[l], s5_log_dt[l], s5_b_re[l], s5_b_im[l],
                                    s5_c_re[l], s5_c_im[l])
        so = _s5(u.reshape(L * bsz, S5_WIDTH), bt, ar, ai, ct, s5_d[l][None], s5_w_glu[l].astype(BF16),
                 bsz, s5_steps)
        h = _out_ffn(h, sb_o, ro, so.reshape(L, bsz * S5_WIDTH), w_out[l].astype(BF16), norm2_g[l][None],
                     w_gate[l].astype(BF16), w_up[l].astype(BF16), w_down[l].astype(BF16), tm)
    return h[:, PAD_FRONT + N_META:]
```

```python
import functools
import math

import jax
import jax.numpy as jnp
from jax import lax
from jax.experimental import pallas as pl
from jax.experimental.pallas import tpu as pltpu

D_MODEL = 1024
HEAD_DIM = 64
SB_HEADS = 6
RET_HEADS = 6
SB_WIDTH = SB_HEADS * HEAD_DIM
RET_WIDTH = RET_HEADS * HEAD_DIM
S5_WIDTH = D_MODEL - SB_WIDTH - RET_WIDTH
S5_GROUP_CH = 16
S5_GROUPS = S5_WIDTH // S5_GROUP_CH
S5_STATE = 64
S5_NSTATE = S5_GROUPS * S5_STATE
D_IN = 3 * SB_WIDTH + 4 * RET_WIDTH + S5_WIDTH
D_FF = 2816
CHUNK = 128
N_META = 16
PAD_FRONT = CHUNK - N_META
ROPE_BASE = 10000.0
RMS_EPS = 1e-6
SB_QUERY_SCALE = HEAD_DIM ** -0.5 * math.log2(math.e)
S5_INTERLEAVE = 4
MASKED_LOG2_WEIGHT = -1e30

LANES = 128
HEAD_PAIRS = SB_HEADS // 2
VMEM_LIMIT_BYTES = 56 * 1024 * 1024

F32 = jnp.float32
BF16 = jnp.bfloat16


def _dot(a, b):
    return jnp.dot(a, b, preferred_element_type=F32)


def _dot_nt(a, b):
    return lax.dot_general(a, b, (((1,), (1,)), ((), ())), preferred_element_type=F32)


def _dot_tn(a, b):
    return lax.dot_general(a, b, (((0,), (0,)), ((), ())), preferred_element_type=F32)


def _split_dot(x, m):
    hi = x.astype(BF16)
    lo = (x - hi.astype(F32)).astype(BF16)
    return _dot(hi, m) + _dot(lo, m)


def _head_split(t, first_head):
    zero = jnp.zeros_like(t)
    return jnp.concatenate([jnp.where(first_head, t, zero), jnp.where(first_head, zero, t)], axis=0)


def _first_head_lanes():
    return lax.broadcasted_iota(jnp.int32, (1, LANES), 1) < HEAD_DIM


def _in_kernel(h_ref, g1_ref, w_ref, hg_ref, bd_ref, cos_ref, sin_ref,
               sq_ref, sk_ref, sv_ref, rq_ref, rk_ref, rv_ref, rg_ref, u_ref):
    x = h_ref[0]
    ms = jnp.mean(x * x, axis=-1, keepdims=True)
    hn = (x * lax.rsqrt(ms + RMS_EPS) * g1_ref[...]).astype(BF16)

    def proj_pair(seg):
        p = _dot(hn, w_ref[:, seg * SB_WIDTH:min((seg + 2) * SB_WIDTH, D_IN)])
        return p[:, :SB_WIDTH], p[:, SB_WIDTH:]

    def head_norm_pair(a, b, gi):
        p = jnp.concatenate([a, b], axis=1)
        sq = (p * p).astype(BF16)
        wide = bd_ref.shape[0]
        m = jnp.concatenate(
            [_dot(sq[:, c:c + wide], bd_ref[...]) for c in range(0, 2 * SB_WIDTH, wide)], axis=1)
        gains = jnp.concatenate([hg_ref[gi:gi + 1, :], hg_ref[gi + 1:gi + 2, :]], axis=1)
        p = p * lax.rsqrt(m + RMS_EPS) * gains
        return p[:, :SB_WIDTH], p[:, SB_WIDTH:]

    lane = lax.broadcasted_iota(jnp.int32, (1, LANES), 1)
    lower = (lane & (HEAD_DIM - 1)) < HEAD_DIM // 2
    cos = cos_ref[...]
    sin = sin_ref[...]

    def rotary(t):
        outs = []
        for c in range(HEAD_PAIRS):
            tc = t[:, c * LANES:(c + 1) * LANES]
            partner = jnp.where(lower, pltpu.roll(tc, LANES - HEAD_DIM // 2, 1),
                                pltpu.roll(tc, HEAD_DIM // 2, 1))
            outs.append(tc * cos + partner * sin)
        return jnp.concatenate(outs, axis=1)

    sq, sk = head_norm_pair(*proj_pair(0), 0)
    sq_ref[0] = (sq * SB_QUERY_SCALE).astype(BF16)
    sk_ref[0] = sk.astype(BF16)
    sv, rq = proj_pair(2)
    sv_ref[0] = sv.astype(BF16)
    rk, rv = proj_pair(4)
    rv_ref[0] = rv.astype(BF16)
    rq, rk = head_norm_pair(rq, rk, 2)
    rq_ref[0] = rotary(rq)
    rk_ref[0] = rotary(rk)
    rg_ref[0], u_ref[...] = proj_pair(6)


def _in_proj(h, g1, w_in, head_gains, bd, cos_t, sin_t, tm):
    bsz, L, _ = h.shape
    grid = (bsz, L // tm)
    row3 = lambda w: pl.BlockSpec((1, tm, w), lambda b, t: (b, t, 0))
    const = lambda shape: pl.BlockSpec(shape, lambda b, t: (0,) * len(shape), pipeline_mode=pl.Buffered(1))
    out_shapes = [
        jax.ShapeDtypeStruct((bsz, L, SB_WIDTH), BF16),
        jax.ShapeDtypeStruct((bsz, L, SB_WIDTH), BF16),
        jax.ShapeDtypeStruct((bsz, L, SB_WIDTH), BF16),
        jax.ShapeDtypeStruct((bsz, L, RET_WIDTH), F32),
        jax.ShapeDtypeStruct((bsz, L, RET_WIDTH), F32),
        jax.ShapeDtypeStruct((bsz, L, RET_WIDTH), BF16),
        jax.ShapeDtypeStruct((bsz, L, RET_WIDTH), F32),
        jax.ShapeDtypeStruct((L, bsz * S5_WIDTH), F32),
    ]
    out_specs = [row3(SB_WIDTH)] * 7 + [pl.BlockSpec((tm, S5_WIDTH), lambda b, t: (t, b))]
    return pl.pallas_call(
        _in_kernel,
        grid=grid,
        in_specs=[
            row3(D_MODEL),
            const((1, D_MODEL)),
            const((D_MODEL, D_IN)),
            const((4, SB_WIDTH)),
            const(bd.shape),
            pl.BlockSpec((tm, LANES), lambda b, t: (t, 0)),
            pl.BlockSpec((tm, LANES), lambda b, t: (t, 0)),
        ],
        out_specs=out_specs,
        out_shape=out_shapes,
        compiler_params=pltpu.CompilerParams(
            dimension_semantics=("arbitrary", "arbitrary"), vmem_limit_bytes=VMEM_LIMIT_BYTES),
        name="in_proj",
    )(h, g1, w_in, head_gains, bd, cos_t, sin_t)


def _sb_kernel(q_ref, k_ref, v_ref, tri_ref, o_ref, acc_ref, c_ref, sp_ref, z_ref, *, tq, tk):
    qi = pl.program_id(1)
    first = _first_head_lanes()
    acc_ref[...] = jnp.zeros_like(acc_ref)
    c_ref[...] = jnp.zeros_like(c_ref)
    n_key_blocks = k_ref.shape[1] // tk
    width = 2 * tk
    t_idx = qi * tq + lax.broadcasted_iota(jnp.int32, (tq, 1), 0)
    s_lane = lax.broadcasted_iota(jnp.int32, (1, width), 1) & (tk - 1)

    n_pairs = ((qi + 1) * (tq // tk) + 1) // 2

    def block_rows(j):
        return pl.ds(pl.multiple_of(jnp.minimum(j, n_key_blocks - 1) * tk, tk), tk)

    def lanes(p):
        return slice(p * LANES, (p + 1) * LANES)

    def score_dots(m, p):
        q = q_ref[0, :, lanes(p)]
        return [_dot_nt(q, _head_split(k_ref[0, block_rows(j), lanes(p)], first)) for j in (2 * m + 1, 2 * m)]

    def score_logs(zs, m, p, slot, causal, padded):
        for half, (z, j) in enumerate(zip(zs, (2 * m + 1, 2 * m))):
            neg_abs = lax.bitcast_convert_type(
                lax.bitcast_convert_type(z, jnp.uint32) | jnp.uint32(0x80000000), F32)
            sp = jnp.maximum(z, 0.0) + jnp.log2(1.0 + jnp.exp2(neg_abs))
            if causal or padded:
                s_idx = j * tk + s_lane
                mask = s_idx >= PAD_FRONT
                if causal:
                    mask = mask & (s_idx < t_idx)
                sp = jnp.where(mask, sp, 0.0)
                z = jnp.where(mask, z, MASKED_LOG2_WEIGHT)
            cols = slice(half * width, (half + 1) * width)
            sp_ref[p, slot, :, cols] = sp.astype(BF16)
            z_ref[p, slot, :, cols] = z

    def weights(slot, p):
        c = c_ref[p]
        ws = []
        for half in range(2):
            cols = slice(half * width, (half + 1) * width)
            later = _dot(sp_ref[p, slot, :, cols], tri_ref[...]) + c
            ws.append(jnp.exp2(z_ref[p, slot, :, cols] - later).astype(BF16))
            c = jnp.concatenate([jnp.broadcast_to(later[:, 0:1], (tq, tk)),
                                 jnp.broadcast_to(later[:, tk:tk + 1], (tq, tk))], axis=1)
        c_ref[p] = c
        return jnp.concatenate(ws, axis=1)

    def accumulate(w, m, p):
        vv = jnp.concatenate(
            [_head_split(v_ref[0, block_rows(j), lanes(p)], first) for j in (2 * m + 1, 2 * m)], axis=0)
        acc_ref[:, lanes(p)] += _dot(w, vv)

    def first_scores(m, slot):
        for p in range(HEAD_PAIRS):
            score_logs(score_dots(m, p), m, p, slot, True, True)

    def step(m, slot, causal, padded):
        for p in range(HEAD_PAIRS):
            score_logs(score_dots(m, p), m, p, slot, causal, padded)
            accumulate(weights(1 - slot, p), m + 1, p)

    for parity in (0, 1):
        @pl.when((n_pairs - 1) % 2 == parity)
        def _():
            first_scores(n_pairs - 1, parity)
            step(n_pairs - 2, 1 - parity, True, True)

    @pl.when(jnp.logical_and(n_pairs >= 4, n_pairs % 2 == 0))
    def _():
        step(n_pairs - 3, 1, False, False)

    n_double = (n_pairs - 3) // 2

    def body(i, carry):
        m = 2 * (n_double - i)
        step(m, 0, False, False)
        step(m - 1, 1, False, False)
        return carry

    lax.fori_loop(0, n_double, body, 0)

    @pl.when(n_pairs >= 3)
    def _():
        step(0, 0, False, True)

    for p in range(HEAD_PAIRS):
        accumulate(weights(0, p), 0, p)
    o_ref[0] = acc_ref[...].astype(BF16)


def _sb_attention(q, k, v, tri, tq, tk):
    bsz, L, _ = q.shape
    assert tq % tk == 0 and tq // tk >= 3, "the pipeline prologue needs two key-block pairs per query block"
    grid = (bsz, L // tq)
    return pl.pallas_call(
        functools.partial(_sb_kernel, tq=tq, tk=tk),
        grid=grid,
        in_specs=[
            pl.BlockSpec((1, tq, SB_WIDTH), lambda b, i: (b, i, 0)),
            pl.BlockSpec((1, L, SB_WIDTH), lambda b, i: (b, 0, 0)),
            pl.BlockSpec((1, L, SB_WIDTH), lambda b, i: (b, 0, 0)),
            pl.BlockSpec((2 * tk, 2 * tk), lambda b, i: (0, 0)),
        ],
        out_specs=pl.BlockSpec((1, tq, SB_WIDTH), lambda b, i: (b, i, 0)),
        out_shape=jax.ShapeDtypeStruct((bsz, L, SB_WIDTH), BF16),
        scratch_shapes=[pltpu.VMEM((tq, SB_WIDTH), F32), pltpu.VMEM((HEAD_PAIRS, tq, 2 * tk), F32),
                        pltpu.VMEM((HEAD_PAIRS, 2, tq, 4 * tk), BF16),
                        pltpu.VMEM((HEAD_PAIRS, 2, tq, 4 * tk), F32)],
        compiler_params=pltpu.CompilerParams(
            dimension_semantics=("arbitrary", "arbitrary"), vmem_limit_bytes=VMEM_LIMIT_BYTES),
        name="sb_attention",
    )(q, k, v, tri)


def _ret_kernel(q_ref, k_ref, v_ref, g_ref, dq_ref, dk_ref, din_ref, cd_ref, go_ref, bd_ref,
                o_ref, s_ref, *, chunks):
    @pl.when(pl.program_id(1) == 0)
    def _():
        s_ref[...] = jnp.zeros_like(s_ref)

    first = _first_head_lanes()
    same_head = (lax.broadcasted_iota(jnp.int32, (LANES, 1), 0) < HEAD_DIM) == first
    for p in range(HEAD_PAIRS):
        lanes = slice(p * LANES, (p + 1) * LANES)
        state = s_ref[p]
        for c in range(chunks):
            rows = slice(c * CHUNK, (c + 1) * CHUNK)
            q = q_ref[0, rows, lanes]
            k = k_ref[0, rows, lanes] * (HEAD_DIM ** -0.5)
            v = v_ref[0, rows, lanes]
            g = g_ref[0, rows, lanes]
            scores = _dot_nt(q.astype(BF16), _head_split(k.astype(BF16), first)) * din_ref[p]
            inner = _dot(scores.astype(BF16), _head_split(v, first))
            cross = _dot((q * dq_ref[p]).astype(BF16), state.astype(BF16))
            upd = _dot_tn((k * dk_ref[p]).astype(BF16), v)
            state = state * cd_ref[p] + jnp.where(same_head, upd, 0.0)
            o = inner + cross
            ms = _split_dot(o * o, bd_ref[...])
            on = o * lax.rsqrt(ms + RMS_EPS) * go_ref[:, lanes]
            o_ref[0, rows, lanes] = (on * (g * jax.nn.sigmoid(g))).astype(BF16)
        s_ref[p] = state


def _retention(q, k, v, g, dq, dk, din, cd, gout, bd, chunks):
    bsz, L, _ = q.shape
    rows = chunks * CHUNK
    grid = (bsz, L // rows)
    blk = pl.BlockSpec((1, rows, RET_WIDTH), lambda b, n: (b, n, 0))
    const = lambda shape: pl.BlockSpec(shape, lambda b, n: (0,) * len(shape))
    return pl.pallas_call(
        functools.partial(_ret_kernel, chunks=chunks),
        grid=grid,
        in_specs=[
            blk, blk, blk, blk,
            const((HEAD_PAIRS, CHUNK, LANES)), const((HEAD_PAIRS, CHUNK, LANES)),
            const((HEAD_PAIRS, CHUNK, 2 * CHUNK)), const((HEAD_PAIRS, 1, LANES)),
            const((1, RET_WIDTH)), const((LANES, LANES)),
        ],
        out_specs=blk,
        out_shape=jax.ShapeDtypeStruct((bsz, L, RET_WIDTH), BF16),
        scratch_shapes=[pltpu.VMEM((HEAD_PAIRS, LANES, LANES), F32)],
        compiler_params=pltpu.CompilerParams(dimension_semantics=("arbitrary", "arbitrary")),
        name="retention",
    )(q, k, v, g, dq, dk, din, cd, gout, bd)


def _s5_kernel(u_ref, u_old_ref, bt_ref, ar_ref, ai_ref, ct_ref, d_ref, wg_ref, o_ref,
               bu0_ref, bu1_ref, x0_ref, x1_ref, st_ref, ut_ref, uo_ref, ot_ref, *, steps, bsz, slab):
    i = pl.program_id(0)
    bu_refs = (bu0_ref, bu1_ref)
    x_refs = (x0_ref, x1_ref)

    @pl.when(i == 0)
    def _():
        for ref in (st_ref, bu0_ref, bu1_ref, x0_ref, x1_ref):
            ref[...] = jnp.zeros_like(ref)

    def body(slot):
        bu_new_ref, bu_ref = bu_refs[slot], bu_refs[1 - slot]
        x_old_ref, x_ref = x_refs[slot], x_refs[1 - slot]
        planes = S5_WIDTH // LANES
        for b in range(bsz):
            for h in range(planes):
                chans = slice(b * S5_WIDTH + h * LANES, b * S5_WIDTH + (h + 1) * LANES)
                ut_ref[h, pl.ds(b, steps, stride=bsz), :] = u_ref[:, chans]
                uo_ref[h, pl.ds(b, steps, stride=bsz), :] = u_old_ref[:, chans]
        u_old = jnp.concatenate([uo_ref[h] for h in range(planes)], axis=1)
        u_new = jnp.concatenate([ut_ref[h] for h in range(planes)], axis=1).astype(BF16)
        y = d_ref[...] * u_old
        slabs = S5_NSTATE // slab
        runs_per_slab = S5_INTERLEAVE // slabs
        run = steps // runs_per_slab
        kw = 2 * S5_NSTATE // S5_INTERLEAVE
        for piece in range(S5_INTERLEAVE):
            cols = slice(piece * kw, (piece + 1) * kw)
            y = y + _dot(x_old_ref[:, cols], ct_ref[cols, :])

            ls, part = divmod(piece, runs_per_slab)
            re = slice(ls * slab, (ls + 1) * slab)
            im = slice(S5_NSTATE + ls * slab, S5_NSTATE + (ls + 1) * slab)
            ar = jnp.broadcast_to(ar_ref[:, re], (bsz, slab))
            ai = jnp.broadcast_to(ai_ref[:, re], (bsz, slab))
            xr = st_ref[:, re]
            xi = st_ref[:, im]
            for t in range(part * run, (part + 1) * run):
                rows = slice(t * bsz, (t + 1) * bsz)
                xr, xi = (ar * xr - ai * xi + bu_ref[rows, re],
                          ar * xi + ai * xr + bu_ref[rows, im])
                x_ref[rows, re] = xr.astype(BF16)
                x_ref[rows, im] = xi.astype(BF16)
            st_ref[:, re] = xr
            st_ref[:, im] = xi

            bu_new_ref[:, cols] = _dot(u_new, bt_ref[:, cols])

        y = jax.nn.gelu(y)
        out = y * jax.nn.sigmoid(_dot(y.astype(BF16), wg_ref[...]))
        for h in range(planes):
            ot_ref[h] = out[:, h * LANES:(h + 1) * LANES]
        for b in range(bsz):
            for h in range(planes):
                chans = slice(b * S5_WIDTH + h * LANES, b * S5_WIDTH + (h + 1) * LANES)
                o_ref[:, chans] = ot_ref[h, pl.ds(b, steps, stride=bsz), :].astype(BF16)

    for slot in (0, 1):
        @pl.when(i % 2 == slot)
        def _():
            body(slot)


def _s5(u, bt, ar, ai, ct, d, wg, bsz, steps):
    L = u.shape[0]
    blk_rows = steps * bsz
    n_blocks = L // steps
    io_block = (steps, bsz * S5_WIDTH)
    const = lambda shape: pl.BlockSpec(shape, lambda t: (0, 0))
    return pl.pallas_call(
        functools.partial(_s5_kernel, steps=steps, bsz=bsz, slab=512),
        grid=(n_blocks + 2,),
        in_specs=[
            pl.BlockSpec(io_block, lambda t: (jnp.minimum(t, n_blocks - 1), 0)),
            pl.BlockSpec(io_block, lambda t: (jnp.maximum(t - 2, 0), 0)),
            const((S5_WIDTH, 2 * S5_NSTATE)),
            const((1, S5_NSTATE)), const((1, S5_NSTATE)),
            const((2 * S5_NSTATE, S5_WIDTH)),
            const((1, S5_WIDTH)),
            const((S5_WIDTH, S5_WIDTH)),
        ],
        out_specs=pl.BlockSpec(io_block, lambda t: (jnp.maximum(t - 2, 0), 0)),
        out_shape=jax.ShapeDtypeStruct((L, bsz * S5_WIDTH), BF16),
        scratch_shapes=[
            pltpu.VMEM((blk_rows, 2 * S5_NSTATE), F32), pltpu.VMEM((blk_rows, 2 * S5_NSTATE), F32),
            pltpu.VMEM((blk_rows, 2 * S5_NSTATE), BF16), pltpu.VMEM((blk_rows, 2 * S5_NSTATE), BF16),
            pltpu.VMEM((bsz, 2 * S5_NSTATE), F32),
            pltpu.VMEM((S5_WIDTH // LANES, blk_rows, LANES), F32),
            pltpu.VMEM((S5_WIDTH // LANES, blk_rows, LANES), F32),
            pltpu.VMEM((S5_WIDTH // LANES, blk_rows, LANES), F32),
        ],
        compiler_params=pltpu.CompilerParams(
            dimension_semantics=("arbitrary",), vmem_limit_bytes=VMEM_LIMIT_BYTES),
        name="s5_mixer",
    )(u, u, bt, ar, ai, ct, d, wg)


FF_TILE_STARTS = (0, 768, 1536, 2304, D_FF)


def _out_ffn_kernel(h_ref, sb_ref, ro_ref, so_ref, wo_ref, g2_ref, wg_ref, wu_ref, wd_ref, o_ref, *, tm):
    mix = jnp.concatenate([sb_ref[0], ro_ref[0], so_ref[...]], axis=1)
    h1 = h_ref[0] + _dot(mix, wo_ref[...])
    ms = jnp.mean(h1 * h1, axis=-1, keepdims=True)
    hn = (h1 * lax.rsqrt(ms + RMS_EPS) * g2_ref[...]).astype(BF16)
    ffn = jnp.zeros_like(h1)
    for lo, hi in zip(FF_TILE_STARTS[:-1], FF_TILE_STARTS[1:]):
        gate = _dot(hn, wg_ref[:, lo:hi])
        up = _dot(hn, wu_ref[:, lo:hi])
        act = (gate * jax.nn.sigmoid(gate)) * up
        ffn = ffn + _dot(act.astype(BF16), wd_ref[lo:hi, :])
    h2 = h1 + ffn
    pos = pl.program_id(1) * tm + lax.broadcasted_iota(jnp.int32, (tm, 1), 0)
    o_ref[0] = jnp.where(pos >= PAD_FRONT, h2, 0.0)


def _out_ffn(h, sb_o, ro, so_tm, w_out, g2, w_gate, w_up, w_down, tm):
    bsz, L, _ = h.shape
    grid = (bsz, L // tm)
    row3 = lambda w: pl.BlockSpec((1, tm, w), lambda b, t: (b, t, 0))
    const = lambda shape: pl.BlockSpec(shape, lambda b, t: (0, 0), pipeline_mode=pl.Buffered(1))
    return pl.pallas_call(
        functools.partial(_out_ffn_kernel, tm=tm),
        grid=grid,
        in_specs=[
            row3(D_MODEL), row3(SB_WIDTH), row3(RET_WIDTH),
            pl.BlockSpec((tm, S5_WIDTH), lambda b, t: (t, b)),
            const((D_MODEL, D_MODEL)), const((1, D_MODEL)),
            const((D_MODEL, D_FF)), const((D_MODEL, D_FF)), const((D_FF, D_MODEL)),
        ],
        out_specs=row3(D_MODEL),
        out_shape=jax.ShapeDtypeStruct((bsz, L, D_MODEL), F32),
        compiler_params=pltpu.CompilerParams(
            dimension_semantics=("arbitrary", "arbitrary"), vmem_limit_bytes=VMEM_LIMIT_BYTES),
        name="out_ffn",
    )(h, sb_o, ro, so_tm, w_out, g2, w_gate, w_up, w_down)


def _block_diag_mean(width):
    idx = jnp.arange(width) // HEAD_DIM
    return jnp.where(idx[:, None] == idx[None, :], 1.0 / HEAD_DIM, 0.0).astype(BF16)


def _suffix_count_matrix(tk):
    idx = jnp.arange(2 * tk)
    same = (idx[:, None] // tk) == (idx[None, :] // tk)
    return jnp.where(same & (idx[:, None] >= idx[None, :]), 1.0, 0.0).astype(BF16)


def _rope_tables(L):
    half = HEAD_DIM // 2
    pos = (jnp.arange(L) - PAD_FRONT).astype(F32)
    inv = ROPE_BASE ** (-jnp.arange(half, dtype=F32) / half)
    ang = pos[:, None] * inv[None, :]
    cos = jnp.cos(ang)
    sin = jnp.sin(ang)
    cos_t = jnp.tile(jnp.concatenate([cos, cos], axis=1), (1, LANES // HEAD_DIM))
    sin_t = jnp.tile(jnp.concatenate([-sin, sin], axis=1), (1, LANES // HEAD_DIM))
    return cos_t, sin_t


def _retention_tables():
    log_gamma = jnp.log1p(-jnp.exp2(-5.0 - jnp.arange(RET_HEADS, dtype=F32)))
    i = jnp.arange(CHUNK, dtype=F32)
    diff = i[:, None] - i[None, :]
    decay_in = jnp.where(diff >= 0, jnp.exp(log_gamma[:, None, None] * jnp.maximum(diff, 0.0)), 0.0)
    q_decay = jnp.exp(log_gamma[:, None] * (i + 1.0))
    k_decay = jnp.exp(log_gamma[:, None] * (CHUNK - 1.0 - i))
    chunk_decay = jnp.exp(log_gamma * CHUNK)
    per_lane = lambda t: jnp.repeat(t.T.reshape(CHUNK, HEAD_PAIRS, 2), HEAD_DIM, axis=2).transpose(1, 0, 2)
    dq = per_lane(q_decay)
    dk = per_lane(k_decay)
    din = decay_in.reshape(HEAD_PAIRS, 2, CHUNK, CHUNK).transpose(0, 2, 1, 3).reshape(HEAD_PAIRS, CHUNK, 2 * CHUNK)
    cd = jnp.repeat(chunk_decay.reshape(HEAD_PAIRS, 1, 2), HEAD_DIM, axis=2)
    return dq, dk, din, cd


def _s5_params(lam_re, lam_im, log_dt, b_re, b_im, c_re, c_im):
    dt = jnp.exp(log_dt)[:, None]
    mag = jnp.exp(lam_re * dt)
    ar = mag * jnp.cos(lam_im * dt)
    ai = mag * jnp.sin(lam_im * dt)
    den = lam_re * lam_re + lam_im * lam_im
    fr = ((ar - 1.0) * lam_re + ai * lam_im) / den
    fi = (ai * lam_re - (ar - 1.0) * lam_im) / den
    bbr = fr[..., None] * b_re - fi[..., None] * b_im
    bbi = fr[..., None] * b_im + fi[..., None] * b_re
    eye = jnp.eye(S5_GROUPS, dtype=F32)
    expand_b = lambda m: jnp.einsum('gpc,gh->gchp', m, eye).reshape(S5_WIDTH, S5_NSTATE)
    bt = jnp.concatenate([expand_b(bbr), expand_b(bbi)], axis=1).astype(BF16)
    expand_c = lambda m: jnp.einsum('gcp,gh->hpgc', m, eye).reshape(S5_NSTATE, S5_WIDTH)
    ct = jnp.concatenate([expand_c(c_re), -expand_c(c_im)], axis=0).astype(BF16)
    return bt, ar.reshape(1, S5_NSTATE), ai.reshape(1, S5_NSTATE), ct


def _largest_tile(L, candidates):
    for c in candidates:
        if L % c == 0:
            return c
    raise ValueError(f"no supported tile divides sequence length {L}")


def kernel(x, meta_tokens, norm1_g, w_in, sb_q_g, sb_k_g, ret_q_g, ret_k_g, ret_out_g, s5_lam_re, s5_lam_im,
           s5_log_dt, s5_b_re, s5_b_im, s5_c_re, s5_c_im, s5_d, s5_w_glu, w_out, norm2_g, w_gate, w_up, w_down):
    bsz, seq, _ = x.shape
    depth = w_in.shape[0]
    L = PAD_FRONT + N_META + seq
    assert L % CHUNK == 0
    tm = _largest_tile(L, (528, 384, 128))
    tq = _largest_tile(L, (384,))
    s5_steps = _largest_tile(L, (32, 16, 8))

    pad = jnp.zeros((bsz, PAD_FRONT, D_MODEL), x.dtype)
    meta = jnp.broadcast_to(meta_tokens[None].astype(x.dtype), (bsz, N_META, D_MODEL))
    h = jnp.concatenate([pad, meta, x], axis=1)

    bd128 = _block_diag_mean(LANES)
    bd256 = _block_diag_mean(2 * LANES)
    tri = _suffix_count_matrix(CHUNK)
    cos_t, sin_t = _rope_tables(L)
    dq, dk, din, cd = _retention_tables()
    tile6 = lambda g: jnp.tile(g, SB_HEADS)

    for l in range(depth):
        head_gains = jnp.stack([tile6(sb_q_g[l]), tile6(sb_k_g[l]), tile6(ret_q_g[l]), tile6(ret_k_g[l])])
        sq, sk, sv, rq, rk, rv, rg, u = _in_proj(
            h, norm1_g[l][None], w_in[l].astype(BF16), head_gains, bd256, cos_t, sin_t, tm)
        sb_o = _sb_attention(sq, sk, sv, tri, tq, CHUNK)
        ro = _retention(rq, rk, rv, rg, dq, dk, din, cd, ret_out_g[l][None], bd128, tq // CHUNK)
        bt, ar, ai, ct = _s5_params(s5_lam_re[l], s5_lam_im[l], s5_log_dt[l], s5_b_re[l], s5_b_im[l],
                                    s5_c_re[l], s5_c_im[l])
        so = _s5(u, bt, ar, ai, ct, s5_d[l][None], s5_w_glu[l].astype(BF16), bsz, s5_steps)
        h = _out_ffn(h, sb_o, ro, so, w_out[l].astype(BF16), norm2_g[l][None],
                     w_gate[l].astype(BF16), w_up[l].astype(BF16), w_down[l].astype(BF16), tm)
    return h[:, PAD_FRONT + N_META:]
```

```python
import functools
import math

import jax
import jax.numpy as jnp
from jax import lax
from jax.experimental import pallas as pl
from jax.experimental.pallas import tpu as pltpu

D_MODEL = 1024
HEAD_DIM = 64
SB_HEADS = 6
RET_HEADS = 6
SB_WIDTH = SB_HEADS * HEAD_DIM
RET_WIDTH = RET_HEADS * HEAD_DIM
S5_WIDTH = D_MODEL - SB_WIDTH - RET_WIDTH
S5_GROUP_CH = 16
S5_GROUPS = S5_WIDTH // S5_GROUP_CH
S5_STATE = 64
S5_NSTATE = S5_GROUPS * S5_STATE
D_IN = 3 * SB_WIDTH + 4 * RET_WIDTH + S5_WIDTH
D_FF = 2816
CHUNK = 128
N_META = 16
PAD_FRONT = CHUNK - N_META
ROPE_BASE = 10000.0
RMS_EPS = 1e-6
SB_QUERY_SCALE = HEAD_DIM ** -0.5 * math.log2(math.e)
S5_INTERLEAVE = 4
MASKED_LOG2_WEIGHT = -1e30

LANES = 128
HEAD_PAIRS = SB_HEADS // 2
VMEM_LIMIT_BYTES = 56 * 1024 * 1024

F32 = jnp.float32
BF16 = jnp.bfloat16


def _dot(a, b):
    return jnp.dot(a, b, preferred_element_type=F32)


def _dot_nt(a, b):
    return lax.dot_general(a, b, (((1,), (1,)), ((), ())), preferred_element_type=F32)


def _dot_tn(a, b):
    return lax.dot_general(a, b, (((0,), (0,)), ((), ())), preferred_element_type=F32)


def _split_dot(x, m):
    hi = x.astype(BF16)
    lo = (x - hi.astype(F32)).astype(BF16)
    return _dot(hi, m) + _dot(lo, m)


def _head_split(t, first_head):
    zero = jnp.zeros_like(t)
    return jnp.concatenate([jnp.where(first_head, t, zero), jnp.where(first_head, zero, t)], axis=0)


def _first_head_lanes():
    return lax.broadcasted_iota(jnp.int32, (1, LANES), 1) < HEAD_DIM


def _in_kernel(h_ref, g1_ref, w_ref, hg_ref, bd_ref, cos_ref, sin_ref,
               sq_ref, sk_ref, sv_ref, rq_ref, rk_ref, rv_ref, rg_ref, u_ref):
    x = h_ref[0]
    ms = jnp.mean(x * x, axis=-1, keepdims=True)
    hn = (x * lax.rsqrt(ms + RMS_EPS) * g1_ref[...]).astype(BF16)

    def proj_pair(seg):
        p = _dot(hn, w_ref[:, seg * SB_WIDTH:min((seg + 2) * SB_WIDTH, D_IN)])
        return p[:, :SB_WIDTH], p[:, SB_WIDTH:]

    def head_norm_pair(a, b, gi):
        p = jnp.concatenate([a, b], axis=1)
        sq = (p * p).astype(BF16)
        wide = bd_ref.shape[0]
        m = jnp.concatenate(
            [_dot(sq[:, c:c + wide], bd_ref[...]) for c in range(0, 2 * SB_WIDTH, wide)], axis=1)
        gains = jnp.concatenate([hg_ref[gi:gi + 1, :], hg_ref[gi + 1:gi + 2, :]], axis=1)
        p = p * lax.rsqrt(m + RMS_EPS) * gains
        return p[:, :SB_WIDTH], p[:, SB_WIDTH:]

    lane = lax.broadcasted_iota(jnp.int32, (1, LANES), 1)
    lower = (lane & (HEAD_DIM - 1)) < HEAD_DIM // 2
    cos = cos_ref[...]
    sin = sin_ref[...]

    def rotary(t):
        outs = []
        for c in range(HEAD_PAIRS):
            tc = t[:, c * LANES:(c + 1) * LANES]
            partner = jnp.where(lower, pltpu.roll(tc, LANES - HEAD_DIM // 2, 1),
                                pltpu.roll(tc, HEAD_DIM // 2, 1))
            outs.append(tc * cos + partner * sin)
        return jnp.concatenate(outs, axis=1)

    sq, sk = head_norm_pair(*proj_pair(0), 0)
    sq_ref[0] = (sq * SB_QUERY_SCALE).astype(BF16)
    sk_ref[0] = sk.astype(BF16)
    sv, rq = proj_pair(2)
    sv_ref[0] = sv.astype(BF16)
    rk, rv = proj_pair(4)
    rv_ref[0] = rv.astype(BF16)
    rq, rk = head_norm_pair(rq, rk, 2)
    rq_ref[0] = rotary(rq)
    rk_ref[0] = rotary(rk)
    rg_ref[0], u_ref[...] = proj_pair(6)


def _in_proj(h, g1, w_in, head_gains, bd, cos_t, sin_t, tm):
    bsz, L, _ = h.shape
    grid = (bsz, L // tm)
    row3 = lambda w: pl.BlockSpec((1, tm, w), lambda b, t: (b, t, 0))
    const = lambda shape: pl.BlockSpec(shape, lambda b, t: (0,) * len(shape), pipeline_mode=pl.Buffered(1))
    out_shapes = [
        jax.ShapeDtypeStruct((bsz, L, SB_WIDTH), BF16),
        jax.ShapeDtypeStruct((bsz, L, SB_WIDTH), BF16),
        jax.ShapeDtypeStruct((bsz, L, SB_WIDTH), BF16),
        jax.ShapeDtypeStruct((bsz, L, RET_WIDTH), F32),
        jax.ShapeDtypeStruct((bsz, L, RET_WIDTH), F32),
        jax.ShapeDtypeStruct((bsz, L, RET_WIDTH), BF16),
        jax.ShapeDtypeStruct((bsz, L, RET_WIDTH), F32),
        jax.ShapeDtypeStruct((L, bsz * S5_WIDTH), F32),
    ]
    out_specs = [row3(SB_WIDTH)] * 7 + [pl.BlockSpec((tm, S5_WIDTH), lambda b, t: (t, b))]
    return pl.pallas_call(
        _in_kernel,
        grid=grid,
        in_specs=[
            row3(D_MODEL),
            const((1, D_MODEL)),
            const((D_MODEL, D_IN)),
            const((4, SB_WIDTH)),
            const(bd.shape),
            pl.BlockSpec((tm, LANES), lambda b, t: (t, 0)),
            pl.BlockSpec((tm, LANES), lambda b, t: (t, 0)),
        ],
        out_specs=out_specs,
        out_shape=out_shapes,
        compiler_params=pltpu.CompilerParams(
            dimension_semantics=("arbitrary", "arbitrary"), vmem_limit_bytes=VMEM_LIMIT_BYTES),
        name="in_proj",
    )(h, g1, w_in, head_gains, bd, cos_t, sin_t)


def _sb_kernel(q_ref, k_ref, v_ref, tri_ref, o_ref, acc_ref, c_ref, sp_ref, z_ref, *, tq, tk):
    qi = pl.program_id(1)
    first = _first_head_lanes()
    acc_ref[...] = jnp.zeros_like(acc_ref)
    c_ref[...] = jnp.zeros_like(c_ref)
    n_key_blocks = k_ref.shape[1] // tk
    width = 2 * tk
    t_idx = qi * tq + lax.broadcasted_iota(jnp.int32, (tq, 1), 0)
    s_lane = lax.broadcasted_iota(jnp.int32, (1, width), 1) & (tk - 1)

    n_pairs = ((qi + 1) * (tq // tk) + 1) // 2

    def block_rows(j):
        return pl.ds(pl.multiple_of(jnp.minimum(j, n_key_blocks - 1) * tk, tk), tk)

    def lanes(p):
        return slice(p * LANES, (p + 1) * LANES)

    def score_dots(m, p):
        q = q_ref[0, :, lanes(p)]
        return [_dot_nt(q, _head_split(k_ref[0, block_rows(j), lanes(p)], first)) for j in (2 * m + 1, 2 * m)]

    def score_logs(zs, m, p, slot, causal, padded):
        for half, (z, j) in enumerate(zip(zs, (2 * m + 1, 2 * m))):
            neg_abs = lax.bitcast_convert_type(
                lax.bitcast_convert_type(z, jnp.uint32) | jnp.uint32(0x80000000), F32)
            sp = jnp.maximum(z, 0.0) + jnp.log2(1.0 + jnp.exp2(neg_abs))
            if causal or padded:
                s_idx = j * tk + s_lane
                mask = s_idx >= PAD_FRONT
                if causal:
                    mask = mask & (s_idx < t_idx)
                sp = jnp.where(mask, sp, 0.0)
                z = jnp.where(mask, z, MASKED_LOG2_WEIGHT)
            cols = slice(half * width, (half + 1) * width)
            sp_ref[p, slot, :, cols] = sp.astype(BF16)
            z_ref[p, slot, :, cols] = z

    def weights(slot, p):
        c = c_ref[p]
        ws = []
        for half in range(2):
            cols = slice(half * width, (half + 1) * width)
            later = _dot(sp_ref[p, slot, :, cols], tri_ref[...]) + c
            ws.append(jnp.exp2(z_ref[p, slot, :, cols] - later).astype(BF16))
            c = jnp.concatenate([jnp.broadcast_to(later[:, 0:1], (tq, tk)),
                                 jnp.broadcast_to(later[:, tk:tk + 1], (tq, tk))], axis=1)
        c_ref[p] = c
        return jnp.concatenate(ws, axis=1)

    def accumulate(w, m, p):
        vv = jnp.concatenate(
            [_head_split(v_ref[0, block_rows(j), lanes(p)], first) for j in (2 * m + 1, 2 * m)], axis=0)
        acc_ref[:, lanes(p)] += _dot(w, vv)

    def first_scores(m, slot):
        for p in range(HEAD_PAIRS):
            score_logs(score_dots(m, p), m, p, slot, True, True)

    def step(m, slot, causal, padded):
        for p in range(HEAD_PAIRS):
            accumulate(weights(1 - slot, p), m + 1, p)
            score_logs(score_dots(m, p), m, p, slot, causal, padded)

    for parity in (0, 1):
        @pl.when((n_pairs - 1) % 2 == parity)
        def _():
            first_scores(n_pairs - 1, parity)
            step(n_pairs - 2, 1 - parity, True, True)

    @pl.when(jnp.logical_and(n_pairs >= 4, n_pairs % 2 == 0))
    def _():
        step(n_pairs - 3, 1, False, False)

    n_double = (n_pairs - 3) // 2

    def body(i, carry):
        m = 2 * (n_double - i)
        step(m, 0, False, False)
        step(m - 1, 1, False, False)
        return carry

    lax.fori_loop(0, n_double, body, 0)

    @pl.when(n_pairs >= 3)
    def _():
        step(0, 0, False, True)

    for p in range(HEAD_PAIRS):
        accumulate(weights(0, p), 0, p)
    o_ref[0] = acc_ref[...].astype(BF16)


def _sb_attention(q, k, v, tri, tq, tk):
    bsz, L, _ = q.shape
    assert tq % tk == 0 and tq // tk >= 3, "the pipeline prologue needs two key-block pairs per query block"
    grid = (bsz, L // tq)
    return pl.pallas_call(
        functools.partial(_sb_kernel, tq=tq, tk=tk),
        grid=grid,
        in_specs=[
            pl.BlockSpec((1, tq, SB_WIDTH), lambda b, i: (b, i, 0)),
            pl.BlockSpec((1, L, SB_WIDTH), lambda b, i: (b, 0, 0)),
            pl.BlockSpec((1, L, SB_WIDTH), lambda b, i: (b, 0, 0)),
            pl.BlockSpec((2 * tk, 2 * tk), lambda b, i: (0, 0)),
        ],
        out_specs=pl.BlockSpec((1, tq, SB_WIDTH), lambda b, i: (b, i, 0)),
        out_shape=jax.ShapeDtypeStruct((bsz, L, SB_WIDTH), BF16),
        scratch_shapes=[pltpu.VMEM((tq, SB_WIDTH), F32), pltpu.VMEM((HEAD_PAIRS, tq, 2 * tk), F32),
                        pltpu.VMEM((HEAD_PAIRS, 2, tq, 4 * tk), BF16),
                        pltpu.VMEM((HEAD_PAIRS, 2, tq, 4 * tk), F32)],
        compiler_params=pltpu.CompilerParams(
            dimension_semantics=("arbitrary", "arbitrary"), vmem_limit_bytes=VMEM_LIMIT_BYTES),
        name="sb_attention",
    )(q, k, v, tri)


def _ret_kernel(q_ref, k_ref, v_ref, g_ref, dq_ref, dk_ref, din_ref, cd_ref, go_ref, bd_ref,
                o_ref, s_ref, *, chunks):
    @pl.when(pl.program_id(1) == 0)
    def _():
        s_ref[...] = jnp.zeros_like(s_ref)

    first = _first_head_lanes()
    same_head = (lax.broadcasted_iota(jnp.int32, (LANES, 1), 0) < HEAD_DIM) == first
    units = [(c, p) for c in range(chunks) for p in range(HEAD_PAIRS)]

    def block(ref, c, p):
        return ref[0, c * CHUNK:(c + 1) * CHUNK, p * LANES:(p + 1) * LANES]

    q = {u: block(q_ref, *u) for u in units}
    k = {u: block(k_ref, *u) * (HEAD_DIM ** -0.5) for u in units}
    v = {u: block(v_ref, *u) for u in units}
    scores = {(c, p): _dot_nt(q[c, p].astype(BF16), _head_split(k[c, p].astype(BF16), first)) * din_ref[p]
              for c, p in units}
    inner = {u: _dot(scores[u].astype(BF16), _head_split(v[u], first)) for u in units}
    upd = {(c, p): _dot_tn((k[c, p] * dk_ref[p]).astype(BF16), v[c, p]) for c, p in units}
    out = {}
    for p in range(HEAD_PAIRS):
        state = s_ref[p]
        for c in range(chunks):
            cross = _dot((q[c, p] * dq_ref[p]).astype(BF16), state.astype(BF16))
            state = state * cd_ref[p] + jnp.where(same_head, upd[c, p], 0.0)
            out[c, p] = inner[c, p] + cross
        s_ref[p] = state
    for c, p in units:
        o = out[c, p]
        ms = _split_dot(o * o, bd_ref[...])
        on = o * lax.rsqrt(ms + RMS_EPS) * go_ref[:, p * LANES:(p + 1) * LANES]
        g = block(g_ref, c, p)
        o_ref[0, c * CHUNK:(c + 1) * CHUNK, p * LANES:(p + 1) * LANES] = (
            on * (g * jax.nn.sigmoid(g))).astype(BF16)


def _retention(q, k, v, g, dq, dk, din, cd, gout, bd, chunks):
    bsz, L, _ = q.shape
    rows = chunks * CHUNK
    grid = (bsz, L // rows)
    blk = pl.BlockSpec((1, rows, RET_WIDTH), lambda b, n: (b, n, 0))
    const = lambda shape: pl.BlockSpec(shape, lambda b, n: (0,) * len(shape))
    return pl.pallas_call(
        functools.partial(_ret_kernel, chunks=chunks),
        grid=grid,
        in_specs=[
            blk, blk, blk, blk,
            const((HEAD_PAIRS, CHUNK, LANES)), const((HEAD_PAIRS, CHUNK, LANES)),
            const((HEAD_PAIRS, CHUNK, 2 * CHUNK)), const((HEAD_PAIRS, 1, LANES)),
            const((1, RET_WIDTH)), const((LANES, LANES)),
        ],
        out_specs=blk,
        out_shape=jax.ShapeDtypeStruct((bsz, L, RET_WIDTH), BF16),
        scratch_shapes=[pltpu.VMEM((HEAD_PAIRS, LANES, LANES), F32)],
        compiler_params=pltpu.CompilerParams(dimension_semantics=("arbitrary", "arbitrary")),
        name="retention",
    )(q, k, v, g, dq, dk, din, cd, gout, bd)


def _s5_kernel(u_ref, u_old_ref, bt_ref, ar_ref, ai_ref, ct_ref, d_ref, wg_ref, o_ref,
               bu0_ref, bu1_ref, x0_ref, x1_ref, st_ref, ut_ref, uo_ref, ot_ref, *, steps, bsz, slab):
    i = pl.program_id(0)
    bu_refs = (bu0_ref, bu1_ref)
    x_refs = (x0_ref, x1_ref)

    @pl.when(i == 0)
    def _():
        for ref in (st_ref, bu0_ref, bu1_ref, x0_ref, x1_ref):
            ref[...] = jnp.zeros_like(ref)

    def body(slot):
        bu_new_ref, bu_ref = bu_refs[slot], bu_refs[1 - slot]
        x_old_ref, x_ref = x_refs[slot], x_refs[1 - slot]
        planes = S5_WIDTH // LANES
        for b in range(bsz):
            for h in range(planes):
                chans = slice(b * S5_WIDTH + h * LANES, b * S5_WIDTH + (h + 1) * LANES)
                ut_ref[h, pl.ds(b, steps, stride=bsz), :] = u_ref[:, chans]
                uo_ref[h, pl.ds(b, steps, stride=bsz), :] = u_old_ref[:, chans]
        u_old = jnp.concatenate([uo_ref[h] for h in range(planes)], axis=1)
        u_new = jnp.concatenate([ut_ref[h] for h in range(planes)], axis=1).astype(BF16)
        y = d_ref[...] * u_old
        slabs = S5_NSTATE // slab
        runs_per_slab = S5_INTERLEAVE // slabs
        run = steps // runs_per_slab
        kw = 2 * S5_NSTATE // S5_INTERLEAVE
        for piece in range(S5_INTERLEAVE):
            ls, part = divmod(piece, runs_per_slab)
            re = slice(ls * slab, (ls + 1) * slab)
            im = slice(S5_NSTATE + ls * slab, S5_NSTATE + (ls + 1) * slab)
            ar = jnp.broadcast_to(ar_ref[:, re], (bsz, slab))
            ai = jnp.broadcast_to(ai_ref[:, re], (bsz, slab))
            xr = st_ref[:, re]
            xi = st_ref[:, im]
            for t in range(part * run, (part + 1) * run):
                rows = slice(t * bsz, (t + 1) * bsz)
                xr, xi = (ar * xr - ai * xi + bu_ref[rows, re],
                          ar * xi + ai * xr + bu_ref[rows, im])
                x_ref[rows, re] = xr.astype(BF16)
                x_ref[rows, im] = xi.astype(BF16)
            st_ref[:, re] = xr
            st_ref[:, im] = xi

            cols = slice(piece * kw, (piece + 1) * kw)
            y = y + _dot(x_old_ref[:, cols], ct_ref[cols, :])
            bu_new_ref[:, cols] = _dot(u_new, bt_ref[:, cols])

        y = jax.nn.gelu(y)
        out = y * jax.nn.sigmoid(_dot(y.astype(BF16), wg_ref[...]))
        for h in range(planes):
            ot_ref[h] = out[:, h * LANES:(h + 1) * LANES]
        for b in range(bsz):
            for h in range(planes):
                chans = slice(b * S5_WIDTH + h * LANES, b * S5_WIDTH + (h + 1) * LANES)
                o_ref[:, chans] = ot_ref[h, pl.ds(b, steps, stride=bsz), :].astype(BF16)

    for slot in (0, 1):
        @pl.when(i % 2 == slot)
        def _():
            body(slot)


def _s5(u, bt, ar, ai, ct, d, wg, bsz, steps):
    L = u.shape[0]
    blk_rows = steps * bsz
    n_blocks = L // steps
    io_block = (steps, bsz * S5_WIDTH)
    const = lambda shape: pl.BlockSpec(shape, lambda t: (0, 0))
    return pl.pallas_call(
        functools.partial(_s5_kernel, steps=steps, bsz=bsz, slab=512),
        grid=(n_blocks + 2,),
        in_specs=[
            pl.BlockSpec(io_block, lambda t: (jnp.minimum(t, n_blocks - 1), 0)),
            pl.BlockSpec(io_block, lambda t: (jnp.maximum(t - 2, 0), 0)),
            const((S5_WIDTH, 2 * S5_NSTATE)),
            const((1, S5_NSTATE)), const((1, S5_NSTATE)),
            const((2 * S5_NSTATE, S5_WIDTH)),
            const((1, S5_WIDTH)),
            const((S5_WIDTH, S5_WIDTH)),
        ],
        out_specs=pl.BlockSpec(io_block, lambda t: (jnp.maximum(t - 2, 0), 0)),
        out_shape=jax.ShapeDtypeStruct((L, bsz * S5_WIDTH), BF16),
        scratch_shapes=[
            pltpu.VMEM((blk_rows, 2 * S5_NSTATE), F32), pltpu.VMEM((blk_rows, 2 * S5_NSTATE), F32),
            pltpu.VMEM((blk_rows, 2 * S5_NSTATE), BF16), pltpu.VMEM((blk_rows, 2 * S5_NSTATE), BF16),
            pltpu.VMEM((bsz, 2 * S5_NSTATE), F32),
            pltpu.VMEM((S5_WIDTH // LANES, blk_rows, LANES), F32),
            pltpu.VMEM((S5_WIDTH // LANES, blk_rows, LANES), F32),
            pltpu.VMEM((S5_WIDTH // LANES, blk_rows, LANES), F32),
        ],
        compiler_params=pltpu.CompilerParams(
            dimension_semantics=("arbitrary",), vmem_limit_bytes=VMEM_LIMIT_BYTES),
        name="s5_mixer",
    )(u, u, bt, ar, ai, ct, d, wg)


FF_TILE_STARTS = (0, 768, 1536, 2304, D_FF)


def _out_ffn_kernel(h_ref, sb_ref, ro_ref, so_ref, wo_ref, g2_ref, wg_ref, wu_ref, wd_ref, o_ref, *, tm):
    mix = jnp.concatenate([sb_ref[0], ro_ref[0], so_ref[...]], axis=1)
    h1 = h_ref[0] + _dot(mix, wo_ref[...])
    ms = jnp.mean(h1 * h1, axis=-1, keepdims=True)
    hn = (h1 * lax.rsqrt(ms + RMS_EPS) * g2_ref[...]).astype(BF16)
    ffn = jnp.zeros_like(h1)
    for lo, hi in zip(FF_TILE_STARTS[:-1], FF_TILE_STARTS[1:]):
        gate = _dot(hn, wg_ref[:, lo:hi])
        up = _dot(hn, wu_ref[:, lo:hi])
        act = (gate * jax.nn.sigmoid(gate)) * up
        ffn = ffn + _dot(act.astype(BF16), wd_ref[lo:hi, :])
    h2 = h1 + ffn
    pos = pl.program_id(1) * tm + lax.broadcasted_iota(jnp.int32, (tm, 1), 0)
    o_ref[0] = jnp.where(pos >= PAD_FRONT, h2, 0.0)


def _out_ffn(h, sb_o, ro, so_tm, w_out, g2, w_gate, w_up, w_down, tm):
    bsz, L, _ = h.shape
    grid = (bsz, L // tm)
    row3 = lambda w: pl.BlockSpec((1, tm, w), lambda b, t: (b, t, 0))
    const = lambda shape: pl.BlockSpec(shape, lambda b, t: (0, 0), pipeline_mode=pl.Buffered(1))
    return pl.pallas_call(
        functools.partial(_out_ffn_kernel, tm=tm),
        grid=grid,
        in_specs=[
            row3(D_MODEL), row3(SB_WIDTH), row3(RET_WIDTH),
            pl.BlockSpec((tm, S5_WIDTH), lambda b, t: (t, b)),
            const((D_MODEL, D_MODEL)), const((1, D_MODEL)),
            const((D_MODEL, D_FF)), const((D_MODEL, D_FF)), const((D_FF, D_MODEL)),
        ],
        out_specs=row3(D_MODEL),
        out_shape=jax.ShapeDtypeStruct((bsz, L, D_MODEL), F32),
        compiler_params=pltpu.CompilerParams(
            dimension_semantics=("arbitrary", "arbitrary"), vmem_limit_bytes=VMEM_LIMIT_BYTES),
        name="out_ffn",
    )(h, sb_o, ro, so_tm, w_out, g2, w_gate, w_up, w_down)


def _block_diag_mean(width):
    idx = jnp.arange(width) // HEAD_DIM
    return jnp.where(idx[:, None] == idx[None, :], 1.0 / HEAD_DIM, 0.0).astype(BF16)


def _suffix_count_matrix(tk):
    idx = jnp.arange(2 * tk)
    same = (idx[:, None] // tk) == (idx[None, :] // tk)
    return jnp.where(same & (idx[:, None] >= idx[None, :]), 1.0, 0.0).astype(BF16)


def _rope_tables(L):
    half = HEAD_DIM // 2
    pos = (jnp.arange(L) - PAD_FRONT).astype(F32)
    inv = ROPE_BASE ** (-jnp.arange(half, dtype=F32) / half)
    ang = pos[:, None] * inv[None, :]
    cos = jnp.cos(ang)
    sin = jnp.sin(ang)
    cos_t = jnp.tile(jnp.concatenate([cos, cos], axis=1), (1, LANES // HEAD_DIM))
    sin_t = jnp.tile(jnp.concatenate([-sin, sin], axis=1), (1, LANES // HEAD_DIM))
    return cos_t, sin_t


def _retention_tables():
    log_gamma = jnp.log1p(-jnp.exp2(-5.0 - jnp.arange(RET_HEADS, dtype=F32)))
    i = jnp.arange(CHUNK, dtype=F32)
    diff = i[:, None] - i[None, :]
    decay_in = jnp.where(diff >= 0, jnp.exp(log_gamma[:, None, None] * jnp.maximum(diff, 0.0)), 0.0)
    q_decay = jnp.exp(log_gamma[:, None] * (i + 1.0))
    k_decay = jnp.exp(log_gamma[:, None] * (CHUNK - 1.0 - i))
    chunk_decay = jnp.exp(log_gamma * CHUNK)
    per_lane = lambda t: jnp.repeat(t.T.reshape(CHUNK, HEAD_PAIRS, 2), HEAD_DIM, axis=2).transpose(1, 0, 2)
    dq = per_lane(q_decay)
    dk = per_lane(k_decay)
    din = decay_in.reshape(HEAD_PAIRS, 2, CHUNK, CHUNK).transpose(0, 2, 1, 3).reshape(HEAD_PAIRS, CHUNK, 2 * CHUNK)
    cd = jnp.repeat(chunk_decay.reshape(HEAD_PAIRS, 1, 2), HEAD_DIM, axis=2)
    return dq, dk, din, cd


def _s5_params(lam_re, lam_im, log_dt, b_re, b_im, c_re, c_im):
    dt = jnp.exp(log_dt)[:, None]
    mag = jnp.exp(lam_re * dt)
    ar = mag * jnp.cos(lam_im * dt)
    ai = mag * jnp.sin(lam_im * dt)
    den = lam_re * lam_re + lam_im * lam_im
    fr = ((ar - 1.0) * lam_re + ai * lam_im) / den
    fi = (ai * lam_re - (ar - 1.0) * lam_im) / den
    bbr = fr[..., None] * b_re - fi[..., None] * b_im
    bbi = fr[..., None] * b_im + fi[..., None] * b_re
    eye = jnp.eye(S5_GROUPS, dtype=F32)
    expand_b = lambda m: jnp.einsum('gpc,gh->gchp', m, eye).reshape(S5_WIDTH, S5_NSTATE)
    bt = jnp.concatenate([expand_b(bbr), expand_b(bbi)], axis=1).astype(BF16)
    expand_c = lambda m: jnp.einsum('gcp,gh->hpgc', m, eye).reshape(S5_NSTATE, S5_WIDTH)
    ct = jnp.concatenate([expand_c(c_re), -expand_c(c_im)], axis=0).astype(BF16)
    return bt, ar.reshape(1, S5_NSTATE), ai.reshape(1, S5_NSTATE), ct


def _largest_tile(L, candidates):
    for c in candidates:
        if L % c == 0:
            return c
    raise ValueError(f"no supported tile divides sequence length {L}")


def kernel(x, meta_tokens, norm1_g, w_in, sb_q_g, sb_k_g, ret_q_g, ret_k_g, ret_out_g, s5_lam_re, s5_lam_im,
           s5_log_dt, s5_b_re, s5_b_im, s5_c_re, s5_c_im, s5_d, s5_w_glu, w_out, norm2_g, w_gate, w_up, w_down):
    bsz, seq, _ = x.shape
    depth = w_in.shape[0]
    L = PAD_FRONT + N_META + seq
    assert L % CHUNK == 0
    tm = _largest_tile(L, (528, 384, 128))
    tq = _largest_tile(L, (384,))
    s5_steps = _largest_tile(L, (32, 16, 8))

    pad = jnp.zeros((bsz, PAD_FRONT, D_MODEL), x.dtype)
    meta = jnp.broadcast_to(meta_tokens[None].astype(x.dtype), (bsz, N_META, D_MODEL))
    h = jnp.concatenate([pad, meta, x], axis=1)

    bd128 = _block_diag_mean(LANES)
    bd256 = _block_diag_mean(2 * LANES)
    tri = _suffix_count_matrix(CHUNK)
    cos_t, sin_t = _rope_tables(L)
    dq, dk, din, cd = _retention_tables()
    tile6 = lambda g: jnp.tile(g, SB_HEADS)

    for l in range(depth):
        head_gains = jnp.stack([tile6(sb_q_g[l]), tile6(sb_k_g[l]), tile6(ret_q_g[l]), tile6(ret_k_g[l])])
        sq, sk, sv, rq, rk, rv, rg, u = _in_proj(
            h, norm1_g[l][None], w_in[l].astype(BF16), head_gains, bd256, cos_t, sin_t, tm)
        sb_o = _sb_attention(sq, sk, sv, tri, tq, CHUNK)
        ro = _retention(rq, rk, rv, rg, dq, dk, din, cd, ret_out_g[l][None], bd128, tq // CHUNK)
        bt, ar, ai, ct = _s5_params(s5_lam_re[l], s5_lam_im[l], s5_log_dt[l], s5_b_re[l], s5_b_im[l],
                                    s5_c_re[l], s5_c_im[l])
        so = _s5(u, bt, ar, ai, ct, s5_d[l][None], s5_w_glu[l].astype(BF16), bsz, s5_steps)
        h = _out_ffn(h, sb_o, ro, so, w_out[l].astype(BF16), norm2_g[l][None],
                     w_gate[l].astype(BF16), w_up[l].astype(BF16), w_down[l].astype(BF16), tm)
    return h[:, PAD_FRONT + N_META:]
```

```python
import functools
import math

import jax
import jax.numpy as jnp
from jax import lax
from jax.experimental import pallas as pl
from jax.experimental.pallas import tpu as pltpu

D_MODEL = 1024
HEAD_DIM = 64
SB_HEADS = 6
RET_HEADS = 6
SB_WIDTH = SB_HEADS * HEAD_DIM
RET_WIDTH = RET_HEADS * HEAD_DIM
S5_WIDTH = D_MODEL - SB_WIDTH - RET_WIDTH
S5_GROUP_CH = 16
S5_GROUPS = S5_WIDTH // S5_GROUP_CH
S5_STATE = 64
S5_NSTATE = S5_GROUPS * S5_STATE
D_IN = 3 * SB_WIDTH + 4 * RET_WIDTH + S5_WIDTH
D_FF = 2816
CHUNK = 128
N_META = 16
PAD_FRONT = CHUNK - N_META
ROPE_BASE = 10000.0
RMS_EPS = 1e-6
SB_QUERY_SCALE = HEAD_DIM ** -0.5 * math.log2(math.e)
S5_INTERLEAVE = 4
MASKED_LOG2_WEIGHT = -1e30

LANES = 128
HEAD_PAIRS = SB_HEADS // 2
VMEM_LIMIT_BYTES = 56 * 1024 * 1024

F32 = jnp.float32
BF16 = jnp.bfloat16


def _dot(a, b):
    return jnp.dot(a, b, preferred_element_type=F32)


def _dot_nt(a, b):
    return lax.dot_general(a, b, (((1,), (1,)), ((), ())), preferred_element_type=F32)


def _dot_tn(a, b):
    return lax.dot_general(a, b, (((0,), (0,)), ((), ())), preferred_element_type=F32)


def _split_dot(x, m):
    hi = x.astype(BF16)
    lo = (x - hi.astype(F32)).astype(BF16)
    return _dot(hi, m) + _dot(lo, m)


def _head_split(t, first_head):
    zero = jnp.zeros_like(t)
    return jnp.concatenate([jnp.where(first_head, t, zero), jnp.where(first_head, zero, t)], axis=0)


def _first_head_lanes():
    return lax.broadcasted_iota(jnp.int32, (1, LANES), 1) < HEAD_DIM


def _in_kernel(h_ref, g1_ref, w_ref, hg_ref, bd_ref, cos_ref, sin_ref,
               sq_ref, sk_ref, sv_ref, rq_ref, rk_ref, rv_ref, rg_ref, u_ref):
    x = h_ref[0]
    ms = jnp.mean(x * x, axis=-1, keepdims=True)
    hn = (x * lax.rsqrt(ms + RMS_EPS) * g1_ref[...]).astype(BF16)

    def proj_pair(seg):
        p = _dot(hn, w_ref[:, seg * SB_WIDTH:min((seg + 2) * SB_WIDTH, D_IN)])
        return p[:, :SB_WIDTH], p[:, SB_WIDTH:]

    def head_norm_pair(a, b, gi):
        p = jnp.concatenate([a, b], axis=1)
        sq = (p * p).astype(BF16)
        wide = bd_ref.shape[0]
        m = jnp.concatenate(
            [_dot(sq[:, c:c + wide], bd_ref[...]) for c in range(0, 2 * SB_WIDTH, wide)], axis=1)
        gains = jnp.concatenate([hg_ref[gi:gi + 1, :], hg_ref[gi + 1:gi + 2, :]], axis=1)
        p = p * lax.rsqrt(m + RMS_EPS) * gains
        return p[:, :SB_WIDTH], p[:, SB_WIDTH:]

    lane = lax.broadcasted_iota(jnp.int32, (1, LANES), 1)
    lower = (lane & (HEAD_DIM - 1)) < HEAD_DIM // 2
    cos = cos_ref[...]
    sin = sin_ref[...]

    def rotary(t):
        outs = []
        for c in range(HEAD_PAIRS):
            tc = t[:, c * LANES:(c + 1) * LANES]
            partner = jnp.where(lower, pltpu.roll(tc, LANES - HEAD_DIM // 2, 1),
                                pltpu.roll(tc, HEAD_DIM // 2, 1))
            outs.append(tc * cos + partner * sin)
        return jnp.concatenate(outs, axis=1)

    sq, sk = head_norm_pair(*proj_pair(0), 0)
    sq_ref[0] = (sq * SB_QUERY_SCALE).astype(BF16)
    sk_ref[0] = sk.astype(BF16)
    sv, rq = proj_pair(2)
    sv_ref[0] = sv.astype(BF16)
    rk, rv = proj_pair(4)
    rv_ref[0] = rv.astype(BF16)
    rq, rk = head_norm_pair(rq, rk, 2)
    rq_ref[0] = rotary(rq)
    rk_ref[0] = rotary(rk)
    rg_ref[0], u_ref[...] = proj_pair(6)


def _in_proj(h, g1, w_in, head_gains, bd, cos_t, sin_t, tm):
    bsz, L, _ = h.shape
    grid = (bsz, L // tm)
    row3 = lambda w: pl.BlockSpec((1, tm, w), lambda b, t: (b, t, 0))
    const = lambda shape: pl.BlockSpec(shape, lambda b, t: (0,) * len(shape), pipeline_mode=pl.Buffered(1))
    out_shapes = [
        jax.ShapeDtypeStruct((bsz, L, SB_WIDTH), BF16),
        jax.ShapeDtypeStruct((bsz, L, SB_WIDTH), BF16),
        jax.ShapeDtypeStruct((bsz, L, SB_WIDTH), BF16),
        jax.ShapeDtypeStruct((bsz, L, RET_WIDTH), F32),
        jax.ShapeDtypeStruct((bsz, L, RET_WIDTH), F32),
        jax.ShapeDtypeStruct((bsz, L, RET_WIDTH), BF16),
        jax.ShapeDtypeStruct((bsz, L, RET_WIDTH), F32),
        jax.ShapeDtypeStruct((L, bsz * S5_WIDTH), F32),
    ]
    out_specs = [row3(SB_WIDTH)] * 7 + [pl.BlockSpec((tm, S5_WIDTH), lambda b, t: (t, b))]
    return pl.pallas_call(
        _in_kernel,
        grid=grid,
        in_specs=[
            row3(D_MODEL),
            const((1, D_MODEL)),
            const((D_MODEL, D_IN)),
            const((4, SB_WIDTH)),
            const(bd.shape),
            pl.BlockSpec((tm, LANES), lambda b, t: (t, 0)),
            pl.BlockSpec((tm, LANES), lambda b, t: (t, 0)),
        ],
        out_specs=out_specs,
        out_shape=out_shapes,
        compiler_params=pltpu.CompilerParams(
            dimension_semantics=("arbitrary", "arbitrary"), vmem_limit_bytes=VMEM_LIMIT_BYTES),
        name="in_proj",
    )(h, g1, w_in, head_gains, bd, cos_t, sin_t)


LAST_SLOT = 2


def _sb_kernel(q_ref, k_ref, v_ref, tri_ref, o_ref, acc_ref, c_ref, accp_ref, cp_ref, sp_ref, z_ref, *, tq, tk):
    qi = pl.program_id(1)
    n_q_blocks = pl.num_programs(1) - 1
    first = _first_head_lanes()
    n_key_blocks = k_ref.shape[1] // tk
    width = 2 * tk
    t_idx = qi * tq + lax.broadcasted_iota(jnp.int32, (tq, 1), 0)
    s_lane = lax.broadcasted_iota(jnp.int32, (1, width), 1) & (tk - 1)

    n_pairs = ((qi + 1) * (tq // tk) + 1) // 2

    def block_rows(j):
        return pl.ds(pl.multiple_of(jnp.minimum(j, n_key_blocks - 1) * tk, tk), tk)

    def lanes(p):
        return slice(p * LANES, (p + 1) * LANES)

    def score_dots(m, p):
        q = q_ref[0, :, lanes(p)]
        return [_dot_nt(q, _head_split(k_ref[0, block_rows(j), lanes(p)], first)) for j in (2 * m + 1, 2 * m)]

    def score_logs(zs, m, p, slot, causal, padded):
        for half, (z, j) in enumerate(zip(zs, (2 * m + 1, 2 * m))):
            neg_abs = lax.bitcast_convert_type(
                lax.bitcast_convert_type(z, jnp.uint32) | jnp.uint32(0x80000000), F32)
            sp = jnp.maximum(z, 0.0) + jnp.log2(1.0 + jnp.exp2(neg_abs))
            if causal or padded:
                s_idx = j * tk + s_lane
                mask = s_idx >= PAD_FRONT
                if causal:
                    mask = mask & (s_idx < t_idx)
                sp = jnp.where(mask, sp, 0.0)
                z = jnp.where(mask, z, MASKED_LOG2_WEIGHT)
            cols = slice(half * width, (half + 1) * width)
            sp_ref[p, slot, :, cols] = sp.astype(BF16)
            z_ref[p, slot, :, cols] = z

    def weights(slot, p, c_in_ref, c_out_ref):
        c = c_in_ref[p]
        ws = []
        for half in range(2):
            cols = slice(half * width, (half + 1) * width)
            later = _dot(sp_ref[p, slot, :, cols], tri_ref[...]) + c
            ws.append(jnp.exp2(z_ref[p, slot, :, cols] - later).astype(BF16))
            c = jnp.concatenate([jnp.broadcast_to(later[:, 0:1], (tq, tk)),
                                 jnp.broadcast_to(later[:, tk:tk + 1], (tq, tk))], axis=1)
        if c_out_ref is not None:
            c_out_ref[p] = c
        return jnp.concatenate(ws, axis=1)

    def weighted_values(w, m, p):
        vv = jnp.concatenate(
            [_head_split(v_ref[0, block_rows(j), lanes(p)], first) for j in (2 * m + 1, 2 * m)], axis=0)
        return _dot(w, vv)

    def step(m, slot, causal, padded, last=False):
        weight_slot = 1 if last else 1 - slot
        c_out_ref, acc_out_ref = (cp_ref, accp_ref) if last else (c_ref, acc_ref)
        for p in range(HEAD_PAIRS):
            score_logs(score_dots(m, p), m, p, slot, causal, padded)
            w = weights(weight_slot, p, c_ref, c_out_ref)
            acc_out_ref[:, lanes(p)] = acc_ref[:, lanes(p)] + weighted_values(w, m + 1, p)

    def finish_previous(p):
        w = weights(LAST_SLOT, p, cp_ref, None)
        o_ref[0, :, lanes(p)] = (accp_ref[:, lanes(p)] + weighted_values(w, 0, p)).astype(BF16)

    @pl.when(qi == 0)
    def _():
        acc_ref[...] = jnp.zeros_like(acc_ref)
        c_ref[...] = jnp.zeros_like(c_ref)
        for p in range(HEAD_PAIRS):
            score_logs(score_dots(1, p), 1, p, 1, True, True)
        step(0, LAST_SLOT, True, True, last=True)

    @pl.when(qi == n_q_blocks)
    def _():
        for p in range(HEAD_PAIRS):
            finish_previous(p)

    @pl.when(jnp.logical_and(qi > 0, qi < n_q_blocks))
    def _():
        for parity in (0, 1):
            @pl.when((n_pairs - 1) % 2 == parity)
            def _():
                acc_ref[...] = jnp.zeros_like(acc_ref)
                c_ref[...] = jnp.zeros_like(c_ref)
                for p in range(HEAD_PAIRS):
                    finish_previous(p)
                    score_logs(score_dots(n_pairs - 1, p), n_pairs - 1, p, parity, True, True)
                step(n_pairs - 2, 1 - parity, True, True)

        @pl.when(n_pairs % 2 == 0)
        def _():
            step(n_pairs - 3, 1, False, False)

        n_double = (n_pairs - 3) // 2

        def body(i, carry):
            m = 2 * (n_double - i)
            step(m, 0, False, False)
            step(m - 1, 1, False, False)
            return carry

        lax.fori_loop(0, n_double, body, 0)
        step(0, LAST_SLOT, False, True, last=True)


def _sb_attention(q, k, v, tri, tq, tk):
    bsz, L, _ = q.shape
    assert tq % tk == 0 and tq // tk >= 3, "the pipeline prologue needs two key-block pairs per query block"
    n_q_blocks = L // tq
    slot_shape = (HEAD_PAIRS, 3, tq, 4 * tk)
    return pl.pallas_call(
        functools.partial(_sb_kernel, tq=tq, tk=tk),
        grid=(bsz, n_q_blocks + 1),
        in_specs=[
            pl.BlockSpec((1, tq, SB_WIDTH), lambda b, i: (b, jnp.minimum(i, n_q_blocks - 1), 0)),
            pl.BlockSpec((1, L, SB_WIDTH), lambda b, i: (b, 0, 0)),
            pl.BlockSpec((1, L, SB_WIDTH), lambda b, i: (b, 0, 0)),
            pl.BlockSpec((2 * tk, 2 * tk), lambda b, i: (0, 0)),
        ],
        out_specs=pl.BlockSpec((1, tq, SB_WIDTH), lambda b, i: (b, jnp.maximum(i - 1, 0), 0)),
        out_shape=jax.ShapeDtypeStruct((bsz, L, SB_WIDTH), BF16),
        scratch_shapes=[pltpu.VMEM((tq, SB_WIDTH), F32), pltpu.VMEM((HEAD_PAIRS, tq, 2 * tk), F32),
                        pltpu.VMEM((tq, SB_WIDTH), F32), pltpu.VMEM((HEAD_PAIRS, tq, 2 * tk), F32),
                        pltpu.VMEM(slot_shape, BF16), pltpu.VMEM(slot_shape, F32)],
        compiler_params=pltpu.CompilerParams(
            dimension_semantics=("arbitrary", "arbitrary"), vmem_limit_bytes=VMEM_LIMIT_BYTES),
        name="sb_attention",
    )(q, k, v, tri)


def _ret_kernel(q_ref, k_ref, v_ref, g_ref, dq_ref, dk_ref, din_ref, cd_ref, go_ref, bd_ref,
                o_ref, s_ref, *, chunks):
    @pl.when(pl.program_id(1) == 0)
    def _():
        s_ref[...] = jnp.zeros_like(s_ref)

    first = _first_head_lanes()
    same_head = (lax.broadcasted_iota(jnp.int32, (LANES, 1), 0) < HEAD_DIM) == first
    units = [(c, p) for c in range(chunks) for p in range(HEAD_PAIRS)]

    def block(ref, c, p):
        return ref[0, c * CHUNK:(c + 1) * CHUNK, p * LANES:(p + 1) * LANES]

    q = {u: block(q_ref, *u) for u in units}
    k = {u: block(k_ref, *u) * (HEAD_DIM ** -0.5) for u in units}
    v = {u: block(v_ref, *u) for u in units}
    scores = {(c, p): _dot_nt(q[c, p].astype(BF16), _head_split(k[c, p].astype(BF16), first)) * din_ref[p]
              for c, p in units}
    inner = {u: _dot(scores[u].astype(BF16), _head_split(v[u], first)) for u in units}
    upd = {(c, p): _dot_tn((k[c, p] * dk_ref[p]).astype(BF16), v[c, p]) for c, p in units}
    out = {}
    for p in range(HEAD_PAIRS):
        state = s_ref[p]
        for c in range(chunks):
            cross = _dot((q[c, p] * dq_ref[p]).astype(BF16), state.astype(BF16))
            state = state * cd_ref[p] + jnp.where(same_head, upd[c, p], 0.0)
            out[c, p] = inner[c, p] + cross
        s_ref[p] = state
    for c, p in units:
        o = out[c, p]
        ms = _split_dot(o * o, bd_ref[...])
        on = o * lax.rsqrt(ms + RMS_EPS) * go_ref[:, p * LANES:(p + 1) * LANES]
        g = block(g_ref, c, p)
        o_ref[0, c * CHUNK:(c + 1) * CHUNK, p * LANES:(p + 1) * LANES] = (
            on * (g * jax.nn.sigmoid(g))).astype(BF16)


def _retention(q, k, v, g, dq, dk, din, cd, gout, bd, chunks):
    bsz, L, _ = q.shape
    rows = chunks * CHUNK
    grid = (bsz, L // rows)
    blk = pl.BlockSpec((1, rows, RET_WIDTH), lambda b, n: (b, n, 0))
    const = lambda shape: pl.BlockSpec(shape, lambda b, n: (0,) * len(shape))
    return pl.pallas_call(
        functools.partial(_ret_kernel, chunks=chunks),
        grid=grid,
        in_specs=[
            blk, blk, blk, blk,
            const((HEAD_PAIRS, CHUNK, LANES)), const((HEAD_PAIRS, CHUNK, LANES)),
            const((HEAD_PAIRS, CHUNK, 2 * CHUNK)), const((HEAD_PAIRS, 1, LANES)),
            const((1, RET_WIDTH)), const((LANES, LANES)),
        ],
        out_specs=blk,
        out_shape=jax.ShapeDtypeStruct((bsz, L, RET_WIDTH), BF16),
        scratch_shapes=[pltpu.VMEM((HEAD_PAIRS, LANES, LANES), F32)],
        compiler_params=pltpu.CompilerParams(dimension_semantics=("arbitrary", "arbitrary")),
        name="retention",
    )(q, k, v, g, dq, dk, din, cd, gout, bd)


def _s5_kernel(u_ref, u_old_ref, bt_ref, ar_ref, ai_ref, ct_ref, d_ref, wg_ref, o_ref,
               bu0_ref, bu1_ref, x0_ref, x1_ref, st_ref, ut_ref, uo_ref, ot_ref, *, steps, bsz, slab):
    i = pl.program_id(0)
    bu_refs = (bu0_ref, bu1_ref)
    x_refs = (x0_ref, x1_ref)

    @pl.when(i == 0)
    def _():
        for ref in (st_ref, bu0_ref, bu1_ref, x0_ref, x1_ref):
            ref[...] = jnp.zeros_like(ref)

    def body(slot):
        bu_new_ref, bu_ref = bu_refs[slot], bu_refs[1 - slot]
        x_old_ref, x_ref = x_refs[slot], x_refs[1 - slot]
        planes = S5_WIDTH // LANES
        for b in range(bsz):
            for h in range(planes):
                chans = slice(b * S5_WIDTH + h * LANES, b * S5_WIDTH + (h + 1) * LANES)
                ut_ref[h, pl.ds(b, steps, stride=bsz), :] = u_ref[:, chans]
                uo_ref[h, pl.ds(b, steps, stride=bsz), :] = u_old_ref[:, chans]
        u_old = jnp.concatenate([uo_ref[h] for h in range(planes)], axis=1)
        u_new = jnp.concatenate([ut_ref[h] for h in range(planes)], axis=1).astype(BF16)
        y = d_ref[...] * u_old
        slabs = S5_NSTATE // slab
        runs_per_slab = S5_INTERLEAVE // slabs
        run = steps // runs_per_slab
        kw = 2 * S5_NSTATE // S5_INTERLEAVE
        for piece in range(S5_INTERLEAVE):
            ls, part = divmod(piece, runs_per_slab)
            re = slice(ls * slab, (ls + 1) * slab)
            im = slice(S5_NSTATE + ls * slab, S5_NSTATE + (ls + 1) * slab)
            ar = jnp.broadcast_to(ar_ref[:, re], (bsz, slab))
            ai = jnp.broadcast_to(ai_ref[:, re], (bsz, slab))
            xr = st_ref[:, re]
            xi = st_ref[:, im]
            for t in range(part * run, (part + 1) * run):
                rows = slice(t * bsz, (t + 1) * bsz)
                xr, xi = (ar * xr - ai * xi + bu_ref[rows, re],
                          ar * xi + ai * xr + bu_ref[rows, im])
                x_ref[rows, re] = xr.astype(BF16)
                x_ref[rows, im] = xi.astype(BF16)
            st_ref[:, re] = xr
            st_ref[:, im] = xi

            cols = slice(piece * kw, (piece + 1) * kw)
            y = y + _dot(x_old_ref[:, cols], ct_ref[cols, :])
            bu_new_ref[:, cols] = _dot(u_new, bt_ref[:, cols])

        y = jax.nn.gelu(y)
        out = y * jax.nn.sigmoid(_dot(y.astype(BF16), wg_ref[...]))
        for h in range(planes):
            ot_ref[h] = out[:, h * LANES:(h + 1) * LANES]
        for b in range(bsz):
            for h in range(planes):
                chans = slice(b * S5_WIDTH + h * LANES, b * S5_WIDTH + (h + 1) * LANES)
                o_ref[:, chans] = ot_ref[h, pl.ds(b, steps, stride=bsz), :].astype(BF16)

    for slot in (0, 1):
        @pl.when(i % 2 == slot)
        def _():
            body(slot)


def _s5(u, bt, ar, ai, ct, d, wg, bsz, steps):
    L = u.shape[0]
    blk_rows = steps * bsz
    n_blocks = L // steps
    io_block = (steps, bsz * S5_WIDTH)
    const = lambda shape: pl.BlockSpec(shape, lambda t: (0, 0))
    return pl.pallas_call(
        functools.partial(_s5_kernel, steps=steps, bsz=bsz, slab=512),
        grid=(n_blocks + 2,),
        in_specs=[
            pl.BlockSpec(io_block, lambda t: (jnp.minimum(t, n_blocks - 1), 0)),
            pl.BlockSpec(io_block, lambda t: (jnp.maximum(t - 2, 0), 0)),
            const((S5_WIDTH, 2 * S5_NSTATE)),
            const((1, S5_NSTATE)), const((1, S5_NSTATE)),
            const((2 * S5_NSTATE, S5_WIDTH)),
            const((1, S5_WIDTH)),
            const((S5_WIDTH, S5_WIDTH)),
        ],
        out_specs=pl.BlockSpec(io_block, lambda t: (jnp.maximum(t - 2, 0), 0)),
        out_shape=jax.ShapeDtypeStruct((L, bsz * S5_WIDTH), BF16),
        scratch_shapes=[
            pltpu.VMEM((blk_rows, 2 * S5_NSTATE), F32), pltpu.VMEM((blk_rows, 2 * S5_NSTATE), F32),
            pltpu.VMEM((blk_rows, 2 * S5_NSTATE), BF16), pltpu.VMEM((blk_rows, 2 * S5_NSTATE), BF16),
            pltpu.VMEM((bsz, 2 * S5_NSTATE), F32),
            pltpu.VMEM((S5_WIDTH // LANES, blk_rows, LANES), F32),
            pltpu.VMEM((S5_WIDTH // LANES, blk_rows, LANES), F32),
            pltpu.VMEM((S5_WIDTH // LANES, blk_rows, LANES), F32),
        ],
        compiler_params=pltpu.CompilerParams(
            dimension_semantics=("arbitrary",), vmem_limit_bytes=VMEM_LIMIT_BYTES),
        name="s5_mixer",
    )(u, u, bt, ar, ai, ct, d, wg)


FF_TILE_STARTS = (0, 768, 1536, 2304, D_FF)
FINAL_ROW_TILE = 512


def _out_ffn_kernel(*refs, rows, parts, drop_prefix):
    n_act = 4 * parts
    h_refs, sb_refs, ro_refs, so_refs = (refs[i * parts:(i + 1) * parts] for i in range(4))
    wo_ref, g2_ref, wg_ref, wu_ref, wd_ref, o_ref = refs[n_act:]
    stack = lambda chunks: chunks[0] if parts == 1 else jnp.concatenate(chunks, axis=0)
    mix = jnp.concatenate([stack([r[0] for r in sb_refs]), stack([r[0] for r in ro_refs]),
                           stack([r[...] for r in so_refs])], axis=1)
    h1 = stack([r[0] for r in h_refs]) + _dot(mix, wo_ref[...])
    ms = jnp.mean(h1 * h1, axis=-1, keepdims=True)
    hn = (h1 * lax.rsqrt(ms + RMS_EPS) * g2_ref[...]).astype(BF16)
    ffn = jnp.zeros_like(h1)
    for lo, hi in zip(FF_TILE_STARTS[:-1], FF_TILE_STARTS[1:]):
        gate = _dot(hn, wg_ref[:, lo:hi])
        up = _dot(hn, wu_ref[:, lo:hi])
        act = (gate * jax.nn.sigmoid(gate)) * up
        ffn = ffn + _dot(act.astype(BF16), wd_ref[lo:hi, :])
    h2 = h1 + ffn
    if drop_prefix:
        o_ref[0] = h2
    else:
        pos = pl.program_id(1) * rows + lax.broadcasted_iota(jnp.int32, (rows, 1), 0)
        o_ref[0] = jnp.where(pos >= PAD_FRONT, h2, 0.0)


def _out_ffn(h, sb_o, ro, so_tm, w_out, g2, w_gate, w_up, w_down, tm, drop_prefix=False):
    bsz, L, _ = h.shape
    if drop_prefix:
        rows, parts, first_chunk = FINAL_ROW_TILE, FINAL_ROW_TILE // CHUNK, 1
        out_rows = L - CHUNK
    else:
        rows, parts, first_chunk = tm, 1, 0
        out_rows = L
    chunk = rows // parts
    act3 = lambda w: [pl.BlockSpec((1, chunk, w), lambda b, t, j=j: (b, first_chunk + parts * t + j, 0))
                      for j in range(parts)]
    so_specs = [pl.BlockSpec((chunk, S5_WIDTH), lambda b, t, j=j: (first_chunk + parts * t + j, b))
                for j in range(parts)]
    const = lambda shape: pl.BlockSpec(shape, lambda b, t: (0, 0), pipeline_mode=pl.Buffered(1))
    return pl.pallas_call(
        functools.partial(_out_ffn_kernel, rows=rows, parts=parts, drop_prefix=drop_prefix),
        grid=(bsz, out_rows // rows),
        in_specs=[
            *act3(D_MODEL), *act3(SB_WIDTH), *act3(RET_WIDTH), *so_specs,
            const((D_MODEL, D_MODEL)), const((1, D_MODEL)),
            const((D_MODEL, D_FF)), const((D_MODEL, D_FF)), const((D_FF, D_MODEL)),
        ],
        out_specs=pl.BlockSpec((1, rows, D_MODEL), lambda b, t: (b, t, 0)),
        out_shape=jax.ShapeDtypeStruct((bsz, out_rows, D_MODEL), F32),
        compiler_params=pltpu.CompilerParams(
            dimension_semantics=("arbitrary", "arbitrary"), vmem_limit_bytes=VMEM_LIMIT_BYTES),
        name="out_ffn",
    )(*([h] * parts), *([sb_o] * parts), *([ro] * parts), *([so_tm] * parts),
      w_out, g2, w_gate, w_up, w_down)


def _block_diag_mean(width):
    idx = jnp.arange(width) // HEAD_DIM
    return jnp.where(idx[:, None] == idx[None, :], 1.0 / HEAD_DIM, 0.0).astype(BF16)


def _suffix_count_matrix(tk):
    idx = jnp.arange(2 * tk)
    same = (idx[:, None] // tk) == (idx[None, :] // tk)
    return jnp.where(same & (idx[:, None] >= idx[None, :]), 1.0, 0.0).astype(BF16)


def _rope_tables(L):
    half = HEAD_DIM // 2
    pos = (jnp.arange(L) - PAD_FRONT).astype(F32)
    inv = ROPE_BASE ** (-jnp.arange(half, dtype=F32) / half)
    ang = pos[:, None] * inv[None, :]
    cos = jnp.cos(ang)
    sin = jnp.sin(ang)
    cos_t = jnp.tile(jnp.concatenate([cos, cos], axis=1), (1, LANES // HEAD_DIM))
    sin_t = jnp.tile(jnp.concatenate([-sin, sin], axis=1), (1, LANES // HEAD_DIM))
    return cos_t, sin_t


def _retention_tables():
    log_gamma = jnp.log1p(-jnp.exp2(-5.0 - jnp.arange(RET_HEADS, dtype=F32)))
    i = jnp.arange(CHUNK, dtype=F32)
    diff = i[:, None] - i[None, :]
    decay_in = jnp.where(diff >= 0, jnp.exp(log_gamma[:, None, None] * jnp.maximum(diff, 0.0)), 0.0)
    q_decay = jnp.exp(log_gamma[:, None] * (i + 1.0))
    k_decay = jnp.exp(log_gamma[:, None] * (CHUNK - 1.0 - i))
    chunk_decay = jnp.exp(log_gamma * CHUNK)
    per_lane = lambda t: jnp.repeat(t.T.reshape(CHUNK, HEAD_PAIRS, 2), HEAD_DIM, axis=2).transpose(1, 0, 2)
    dq = per_lane(q_decay)
    dk = per_lane(k_decay)
    din = decay_in.reshape(HEAD_PAIRS, 2, CHUNK, CHUNK).transpose(0, 2, 1, 3).reshape(HEAD_PAIRS, CHUNK, 2 * CHUNK)
    cd = jnp.repeat(chunk_decay.reshape(HEAD_PAIRS, 1, 2), HEAD_DIM, axis=2)
    return dq, dk, din, cd


def _s5_params(lam_re, lam_im, log_dt, b_re, b_im, c_re, c_im):
    dt = jnp.exp(log_dt)[:, None]
    mag = jnp.exp(lam_re * dt)
    ar = mag * jnp.cos(lam_im * dt)
    ai = mag * jnp.sin(lam_im * dt)
    den = lam_re * lam_re + lam_im * lam_im
    fr = ((ar - 1.0) * lam_re + ai * lam_im) / den
    fi = (ai * lam_re - (ar - 1.0) * lam_im) / den
    bbr = fr[..., None] * b_re - fi[..., None] * b_im
    bbi = fr[..., None] * b_im + fi[..., None] * b_re
    eye = jnp.eye(S5_GROUPS, dtype=F32)
    expand_b = lambda m: jnp.einsum('gpc,gh->gchp', m, eye).reshape(S5_WIDTH, S5_NSTATE)
    bt = jnp.concatenate([expand_b(bbr), expand_b(bbi)], axis=1).astype(BF16)
    expand_c = lambda m: jnp.einsum('gcp,gh->hpgc', m, eye).reshape(S5_NSTATE, S5_WIDTH)
    ct = jnp.concatenate([expand_c(c_re), -expand_c(c_im)], axis=0).astype(BF16)
    return bt, ar.reshape(1, S5_NSTATE), ai.reshape(1, S5_NSTATE), ct


def _largest_tile(L, candidates):
    for c in candidates:
        if L % c == 0:
            return c
    raise ValueError(f"no supported tile divides sequence length {L}")


def kernel(x, meta_tokens, norm1_g, w_in, sb_q_g, sb_k_g, ret_q_g, ret_k_g, ret_out_g, s5_lam_re, s5_lam_im,
           s5_log_dt, s5_b_re, s5_b_im, s5_c_re, s5_c_im, s5_d, s5_w_glu, w_out, norm2_g, w_gate, w_up, w_down):
    bsz, seq, _ = x.shape
    depth = w_in.shape[0]
    L = PAD_FRONT + N_META + seq
    assert L % CHUNK == 0
    tm = _largest_tile(L, (528, 384, 128))
    tq = _largest_tile(L, (384,))
    s5_steps = _largest_tile(L, (32, 16, 8))

    pad = jnp.zeros((bsz, PAD_FRONT, D_MODEL), x.dtype)
    meta = jnp.broadcast_to(meta_tokens[None].astype(x.dtype), (bsz, N_META, D_MODEL))
    h = jnp.concatenate([pad, meta, x], axis=1)

    bd128 = _block_diag_mean(LANES)
    bd256 = _block_diag_mean(2 * LANES)
    tri = _suffix_count_matrix(CHUNK)
    cos_t, sin_t = _rope_tables(L)
    dq, dk, din, cd = _retention_tables()
    tile6 = lambda g: jnp.tile(g, SB_HEADS)

    for l in range(depth):
        head_gains = jnp.stack([tile6(sb_q_g[l]), tile6(sb_k_g[l]), tile6(ret_q_g[l]), tile6(ret_k_g[l])])
        sq, sk, sv, rq, rk, rv, rg, u = _in_proj(
            h, norm1_g[l][None], w_in[l].astype(BF16), head_gains, bd256, cos_t, sin_t, tm)
        sb_o = _sb_attention(sq, sk, sv, tri, tq, CHUNK)
        ro = _retention(rq, rk, rv, rg, dq, dk, din, cd, ret_out_g[l][None], bd128, tq // CHUNK)
        bt, ar, ai, ct = _s5_params(s5_lam_re[l], s5_lam_im[l], s5_log_dt[l], s5_b_re[l], s5_b_im[l],
                                    s5_c_re[l], s5_c_im[l])
        so = _s5(u, bt, ar, ai, ct, s5_d[l][None], s5_w_glu[l].astype(BF16), bsz, s5_steps)
        h = _out_ffn(h, sb_o, ro, so, w_out[l].astype(BF16), norm2_g[l][None],
                     w_gate[l].astype(BF16), w_up[l].astype(BF16), w_down[l].astype(BF16), tm,
                     drop_prefix=(l == depth - 1 and seq % FINAL_ROW_TILE == 0))
    return h if h.shape[1] == seq else h[:, PAD_FRONT + N_META:]
```

```python
import functools
import math

import jax
import jax.numpy as jnp
from jax import lax
from jax.experimental import pallas as pl
from jax.experimental.pallas import tpu as pltpu

D_MODEL = 1024
HEAD_DIM = 64
SB_HEADS = 6
RET_HEADS = 6
SB_WIDTH = SB_HEADS * HEAD_DIM
RET_WIDTH = RET_HEADS * HEAD_DIM
S5_WIDTH = D_MODEL - SB_WIDTH - RET_WIDTH
S5_GROUP_CH = 16
S5_GROUPS = S5_WIDTH // S5_GROUP_CH
S5_STATE = 64
S5_NSTATE = S5_GROUPS * S5_STATE
D_IN = 3 * SB_WIDTH + 4 * RET_WIDTH + S5_WIDTH
D_FF = 2816
CHUNK = 128
N_META = 16
PAD_FRONT = CHUNK - N_META
ROPE_BASE = 10000.0
RMS_EPS = 1e-6
SB_QUERY_SCALE = HEAD_DIM ** -0.5 * math.log2(math.e)
S5_INTERLEAVE = 4
MASKED_LOG2_WEIGHT = -1e30

LANES = 128
HEAD_PAIRS = SB_HEADS // 2
VMEM_LIMIT_BYTES = 56 * 1024 * 1024

F32 = jnp.float32
BF16 = jnp.bfloat16


def _dot(a, b):
    return jnp.dot(a, b, preferred_element_type=F32)


def _dot_nt(a, b):
    return lax.dot_general(a, b, (((1,), (1,)), ((), ())), preferred_element_type=F32)


def _dot_tn(a, b):
    return lax.dot_general(a, b, (((0,), (0,)), ((), ())), preferred_element_type=F32)


def _split_dot(x, m):
    hi = x.astype(BF16)
    lo = (x - hi.astype(F32)).astype(BF16)
    return _dot(hi, m) + _dot(lo, m)


def _head_split(t, first_head):
    zero = jnp.zeros_like(t)
    return jnp.concatenate([jnp.where(first_head, t, zero), jnp.where(first_head, zero, t)], axis=0)


def _first_head_lanes():
    return lax.broadcasted_iota(jnp.int32, (1, LANES), 1) < HEAD_DIM


TOKEN_TILE = 3 * CHUNK


def _in_kernel(*refs, from_tokens):
    if from_tokens:
        parts = TOKEN_TILE // CHUNK
        x_refs, prefix_ref, refs = refs[:parts], refs[parts], refs[parts + 1:]
        head = jnp.where(pl.program_id(1) == 0, prefix_ref[...], x_refs[0][0])
        x = jnp.concatenate([head] + [r[0] for r in x_refs[1:]], axis=0)
        refs[-1][0] = x
        refs = refs[:-1]
    else:
        x = refs[0][0]
        refs = refs[1:]
    (g1_ref, w_ref, hg_ref, bd_ref, cos_ref, sin_ref,
     sq_ref, sk_ref, sv_ref, rq_ref, rk_ref, rv_ref, rg_ref, u_ref) = refs
    ms = jnp.mean(x * x, axis=-1, keepdims=True)
    hn = (x * lax.rsqrt(ms + RMS_EPS) * g1_ref[...]).astype(BF16)

    def proj_pair(seg):
        p = _dot(hn, w_ref[:, seg * SB_WIDTH:min((seg + 2) * SB_WIDTH, D_IN)])
        return p[:, :SB_WIDTH], p[:, SB_WIDTH:]

    def head_norm_pair(a, b, gi):
        p = jnp.concatenate([a, b], axis=1)
        sq = (p * p).astype(BF16)
        wide = bd_ref.shape[0]
        m = jnp.concatenate(
            [_dot(sq[:, c:c + wide], bd_ref[...]) for c in range(0, 2 * SB_WIDTH, wide)], axis=1)
        gains = jnp.concatenate([hg_ref[gi:gi + 1, :], hg_ref[gi + 1:gi + 2, :]], axis=1)
        p = p * lax.rsqrt(m + RMS_EPS) * gains
        return p[:, :SB_WIDTH], p[:, SB_WIDTH:]

    lane = lax.broadcasted_iota(jnp.int32, (1, LANES), 1)
    lower = (lane & (HEAD_DIM - 1)) < HEAD_DIM // 2
    cos = cos_ref[...]
    sin = sin_ref[...]

    def rotary(t):
        outs = []
        for c in range(HEAD_PAIRS):
            tc = t[:, c * LANES:(c + 1) * LANES]
            partner = jnp.where(lower, pltpu.roll(tc, LANES - HEAD_DIM // 2, 1),
                                pltpu.roll(tc, HEAD_DIM // 2, 1))
            outs.append(tc * cos + partner * sin)
        return jnp.concatenate(outs, axis=1)

    wide_lane = lax.broadcasted_iota(jnp.int32, (1, SB_WIDTH), 1)
    first_head = (wide_lane & (LANES - 1)) < HEAD_DIM

    def store_per_head(ref, t):
        t = t.astype(BF16)
        zero = jnp.zeros_like(t)
        ref[0, 0] = jnp.where(first_head, t, zero)
        ref[0, 1] = jnp.where(first_head, zero, t)

    sq, sk = head_norm_pair(*proj_pair(0), 0)
    sq_ref[0] = (sq * SB_QUERY_SCALE).astype(BF16)
    store_per_head(sk_ref, sk)
    sv, rq = proj_pair(2)
    store_per_head(sv_ref, sv)
    rk, rv = proj_pair(4)
    rv_ref[0] = rv.astype(BF16)
    rq, rk = head_norm_pair(rq, rk, 2)
    rq_ref[0] = rotary(rq)
    rk_ref[0] = rotary(rk)
    rg_ref[0], u_ref[...] = proj_pair(6)


def _in_proj(h, g1, w_in, head_gains, bd, cos_t, sin_t, tm, prefix=None):
    from_tokens = prefix is not None
    bsz = h.shape[0]
    if from_tokens:
        tm = TOKEN_TILE
        L = h.shape[1] + CHUNK
        parts = tm // CHUNK
        act_specs = [pl.BlockSpec((1, CHUNK, D_MODEL), lambda b, t, j=j: (b, jnp.maximum(parts * t - 1 + j, 0), 0))
                     for j in range(parts)]
        act_specs.append(pl.BlockSpec((CHUNK, D_MODEL), lambda b, t: (0, 0)))
        acts = [h] * parts + [prefix]
    else:
        L = h.shape[1]
        act_specs = [pl.BlockSpec((1, tm, D_MODEL), lambda b, t: (b, t, 0))]
        acts = [h]
    grid = (bsz, L // tm)
    row3 = lambda w: pl.BlockSpec((1, tm, w), lambda b, t: (b, t, 0))
    const = lambda shape: pl.BlockSpec(shape, lambda b, t: (0,) * len(shape), pipeline_mode=pl.Buffered(1))
    out_shapes = [
        jax.ShapeDtypeStruct((bsz, L, SB_WIDTH), BF16),
        jax.ShapeDtypeStruct((bsz, 2, L, SB_WIDTH), BF16),
        jax.ShapeDtypeStruct((bsz, 2, L, SB_WIDTH), BF16),
        jax.ShapeDtypeStruct((bsz, L, RET_WIDTH), F32),
        jax.ShapeDtypeStruct((bsz, L, RET_WIDTH), F32),
        jax.ShapeDtypeStruct((bsz, L, RET_WIDTH), BF16),
        jax.ShapeDtypeStruct((bsz, L, RET_WIDTH), F32),
        jax.ShapeDtypeStruct((L, bsz * S5_WIDTH), F32),
    ]
    per_head = pl.BlockSpec((1, 2, tm, SB_WIDTH), lambda b, t: (b, 0, t, 0))
    out_specs = ([row3(SB_WIDTH), per_head, per_head] + [row3(RET_WIDTH)] * 4
                 + [pl.BlockSpec((tm, S5_WIDTH), lambda b, t: (t, b))])
    if from_tokens:
        out_shapes.append(jax.ShapeDtypeStruct((bsz, L, D_MODEL), F32))
        out_specs.append(row3(D_MODEL))
    return pl.pallas_call(
        functools.partial(_in_kernel, from_tokens=from_tokens),
        grid=grid,
        in_specs=[
            *act_specs,
            const((1, D_MODEL)),
            const((D_MODEL, D_IN)),
            const((4, SB_WIDTH)),
            const(bd.shape),
            pl.BlockSpec((tm, LANES), lambda b, t: (t, 0)),
            pl.BlockSpec((tm, LANES), lambda b, t: (t, 0)),
        ],
        out_specs=out_specs,
        out_shape=out_shapes,
        compiler_params=pltpu.CompilerParams(
            dimension_semantics=("arbitrary", "arbitrary"), vmem_limit_bytes=VMEM_LIMIT_BYTES),
        name="in_proj",
    )(*acts, g1, w_in, head_gains, bd, cos_t, sin_t)


LAST_SLOT = 2


def _sb_kernel(q_ref, k_ref, v_ref, tri_ref, o_ref, acc_ref, c_ref, accp_ref, cp_ref, sp_ref, z_ref, *, tq, tk):
    qi = pl.program_id(1)
    n_q_blocks = pl.num_programs(1) - 1
    n_key_blocks = k_ref.shape[2] // tk
    width = 2 * tk
    t_idx = qi * tq + lax.broadcasted_iota(jnp.int32, (tq, 1), 0)
    s_lane = lax.broadcasted_iota(jnp.int32, (1, width), 1) & (tk - 1)

    n_pairs = ((qi + 1) * (tq // tk) + 1) // 2

    def block_rows(j):
        return pl.ds(pl.multiple_of(jnp.minimum(j, n_key_blocks - 1) * tk, tk), tk)

    def lanes(p):
        return slice(p * LANES, (p + 1) * LANES)

    def per_head_rows(ref, j, p):
        return jnp.concatenate([ref[0, 0, block_rows(j), lanes(p)], ref[0, 1, block_rows(j), lanes(p)]], axis=0)

    def score_dots(m, p):
        q = q_ref[0, :, lanes(p)]
        return [_dot_nt(q, per_head_rows(k_ref, j, p)) for j in (2 * m + 1, 2 * m)]

    def score_logs(zs, m, p, slot, causal, padded):
        for half, (z, j) in enumerate(zip(zs, (2 * m + 1, 2 * m))):
            neg_abs = lax.bitcast_convert_type(
                lax.bitcast_convert_type(z, jnp.uint32) | jnp.uint32(0x80000000), F32)
            sp = jnp.maximum(z, 0.0) + jnp.log2(1.0 + jnp.exp2(neg_abs))
            if causal or padded:
                s_idx = j * tk + s_lane
                mask = s_idx >= PAD_FRONT
                if causal:
                    mask = mask & (s_idx < t_idx)
                sp = jnp.where(mask, sp, 0.0)
                z = jnp.where(mask, z, MASKED_LOG2_WEIGHT)
            cols = slice(half * width, (half + 1) * width)
            sp_ref[p, slot, :, cols] = sp.astype(BF16)
            z_ref[p, slot, :, cols] = z

    def weights(slot, p, c_in_ref, c_out_ref):
        c = c_in_ref[p]
        ws = []
        for half in range(2):
            cols = slice(half * width, (half + 1) * width)
            later = _dot(sp_ref[p, slot, :, cols], tri_ref[...]) + c
            ws.append(jnp.exp2(z_ref[p, slot, :, cols] - later).astype(BF16))
            c = jnp.concatenate([jnp.broadcast_to(later[:, 0:1], (tq, tk)),
                                 jnp.broadcast_to(later[:, tk:tk + 1], (tq, tk))], axis=1)
        if c_out_ref is not None:
            c_out_ref[p] = c
        return jnp.concatenate(ws, axis=1)

    def weighted_values(w, m, p):
        vv = jnp.concatenate([per_head_rows(v_ref, j, p) for j in (2 * m + 1, 2 * m)], axis=0)
        return _dot(w, vv)

    def step(m, slot, causal, padded, last=False):
        weight_slot = 1 if last else 1 - slot
        c_out_ref, acc_out_ref = (cp_ref, accp_ref) if last else (c_ref, acc_ref)
        for p in range(HEAD_PAIRS):
            score_logs(score_dots(m, p), m, p, slot, causal, padded)
            w = weights(weight_slot, p, c_ref, c_out_ref)
            acc_out_ref[:, lanes(p)] = acc_ref[:, lanes(p)] + weighted_values(w, m + 1, p)

    def finish_previous(p):
        w = weights(LAST_SLOT, p, cp_ref, None)
        o_ref[0, :, lanes(p)] = (accp_ref[:, lanes(p)] + weighted_values(w, 0, p)).astype(BF16)

    @pl.when(qi == 0)
    def _():
        acc_ref[...] = jnp.zeros_like(acc_ref)
        c_ref[...] = jnp.zeros_like(c_ref)
        for p in range(HEAD_PAIRS):
            score_logs(score_dots(1, p), 1, p, 1, True, True)
        step(0, LAST_SLOT, True, True, last=True)

    @pl.when(qi == n_q_blocks)
    def _():
        for p in range(HEAD_PAIRS):
            finish_previous(p)

    @pl.when(jnp.logical_and(qi > 0, qi < n_q_blocks))
    def _():
        for parity in (0, 1):
            @pl.when((n_pairs - 1) % 2 == parity)
            def _():
                acc_ref[...] = jnp.zeros_like(acc_ref)
                c_ref[...] = jnp.zeros_like(c_ref)
                for p in range(HEAD_PAIRS):
                    finish_previous(p)
                    score_logs(score_dots(n_pairs - 1, p), n_pairs - 1, p, parity, True, True)
                step(n_pairs - 2, 1 - parity, True, True)

        @pl.when(n_pairs % 2 == 0)
        def _():
            step(n_pairs - 3, 1, False, False)

        n_double = (n_pairs - 3) // 2

        def body(i, carry):
            m = 2 * (n_double - i)
            step(m, 0, False, False)
            step(m - 1, 1, False, False)
            return carry

        lax.fori_loop(0, n_double, body, 0)
        step(0, LAST_SLOT, False, True, last=True)


def _sb_attention(q, k, v, tri, tq, tk):
    bsz, L, _ = q.shape
    assert tq % tk == 0 and tq // tk >= 3, "the pipeline prologue needs two key-block pairs per query block"
    n_q_blocks = L // tq
    slot_shape = (HEAD_PAIRS, 3, tq, 4 * tk)
    return pl.pallas_call(
        functools.partial(_sb_kernel, tq=tq, tk=tk),
        grid=(bsz, n_q_blocks + 1),
        in_specs=[
            pl.BlockSpec((1, tq, SB_WIDTH), lambda b, i: (b, jnp.minimum(i, n_q_blocks - 1), 0)),
            pl.BlockSpec((1, 2, L, SB_WIDTH), lambda b, i: (b, 0, 0, 0)),
            pl.BlockSpec((1, 2, L, SB_WIDTH), lambda b, i: (b, 0, 0, 0)),
            pl.BlockSpec((2 * tk, 2 * tk), lambda b, i: (0, 0)),
        ],
        out_specs=pl.BlockSpec((1, tq, SB_WIDTH), lambda b, i: (b, jnp.maximum(i - 1, 0), 0)),
        out_shape=jax.ShapeDtypeStruct((bsz, L, SB_WIDTH), BF16),
        scratch_shapes=[pltpu.VMEM((tq, SB_WIDTH), F32), pltpu.VMEM((HEAD_PAIRS, tq, 2 * tk), F32),
                        pltpu.VMEM((tq, SB_WIDTH), F32), pltpu.VMEM((HEAD_PAIRS, tq, 2 * tk), F32),
                        pltpu.VMEM(slot_shape, BF16), pltpu.VMEM(slot_shape, F32)],
        compiler_params=pltpu.CompilerParams(
            dimension_semantics=("arbitrary", "arbitrary"), vmem_limit_bytes=VMEM_LIMIT_BYTES),
        name="sb_attention",
    )(q, k, v, tri)


def _ret_kernel(q_ref, k_ref, v_ref, g_ref, dq_ref, dk_ref, din_ref, cd_ref, go_ref, bd_ref,
                o_ref, s_ref, *, chunks):
    @pl.when(pl.program_id(1) == 0)
    def _():
        s_ref[...] = jnp.zeros_like(s_ref)

    first = _first_head_lanes()
    same_head = (lax.broadcasted_iota(jnp.int32, (LANES, 1), 0) < HEAD_DIM) == first
    units = [(c, p) for c in range(chunks) for p in range(HEAD_PAIRS)]

    def block(ref, c, p):
        return ref[0, c * CHUNK:(c + 1) * CHUNK, p * LANES:(p + 1) * LANES]

    q = {u: block(q_ref, *u) for u in units}
    k = {u: block(k_ref, *u) * (HEAD_DIM ** -0.5) for u in units}
    v = {u: block(v_ref, *u) for u in units}
    scores = {(c, p): _dot_nt(q[c, p].astype(BF16), _head_split(k[c, p].astype(BF16), first)) * din_ref[p]
              for c, p in units}
    inner = {u: _dot(scores[u].astype(BF16), _head_split(v[u], first)) for u in units}
    upd = {(c, p): _dot_tn((k[c, p] * dk_ref[p]).astype(BF16), v[c, p]) for c, p in units}
    out = {}
    for p in range(HEAD_PAIRS):
        state = s_ref[p]
        for c in range(chunks):
            cross = _dot((q[c, p] * dq_ref[p]).astype(BF16), state.astype(BF16))
            state = state * cd_ref[p] + jnp.where(same_head, upd[c, p], 0.0)
            out[c, p] = inner[c, p] + cross
        s_ref[p] = state
    for c, p in units:
        o = out[c, p]
        ms = _split_dot(o * o, bd_ref[...])
        on = o * lax.rsqrt(ms + RMS_EPS) * go_ref[:, p * LANES:(p + 1) * LANES]
        g = block(g_ref, c, p)
        o_ref[0, c * CHUNK:(c + 1) * CHUNK, p * LANES:(p + 1) * LANES] = (
            on * (g * jax.nn.sigmoid(g))).astype(BF16)


def _retention(q, k, v, g, dq, dk, din, cd, gout, bd, chunks):
    bsz, L, _ = q.shape
    rows = chunks * CHUNK
    grid = (bsz, L // rows)
    blk = pl.BlockSpec((1, rows, RET_WIDTH), lambda b, n: (b, n, 0))
    const = lambda shape: pl.BlockSpec(shape, lambda b, n: (0,) * len(shape))
    return pl.pallas_call(
        functools.partial(_ret_kernel, chunks=chunks),
        grid=grid,
        in_specs=[
            blk, blk, blk, blk,
            const((HEAD_PAIRS, CHUNK, LANES)), const((HEAD_PAIRS, CHUNK, LANES)),
            const((HEAD_PAIRS, CHUNK, 2 * CHUNK)), const((HEAD_PAIRS, 1, LANES)),
            const((1, RET_WIDTH)), const((LANES, LANES)),
        ],
        out_specs=blk,
        out_shape=jax.ShapeDtypeStruct((bsz, L, RET_WIDTH), BF16),
        scratch_shapes=[pltpu.VMEM((HEAD_PAIRS, LANES, LANES), F32)],
        compiler_params=pltpu.CompilerParams(dimension_semantics=("arbitrary", "arbitrary")),
        name="retention",
    )(q, k, v, g, dq, dk, din, cd, gout, bd)


def _s5_kernel(u_ref, u_old_ref, bt_ref, ar_ref, ai_ref, ct_ref, d_ref, wg_ref, o_ref,
               bu0_ref, bu1_ref, x0_ref, x1_ref, st_ref, ut_ref, uo_ref, ot_ref, *, steps, bsz, slab):
    i = pl.program_id(0)
    bu_refs = (bu0_ref, bu1_ref)
    x_refs = (x0_ref, x1_ref)

    @pl.when(i == 0)
    def _():
        for ref in (st_ref, bu0_ref, bu1_ref, x0_ref, x1_ref):
            ref[...] = jnp.zeros_like(ref)

    def body(slot):
        bu_new_ref, bu_ref = bu_refs[slot], bu_refs[1 - slot]
        x_old_ref, x_ref = x_refs[slot], x_refs[1 - slot]
        planes = S5_WIDTH // LANES
        for b in range(bsz):
            for h in range(planes):
                chans = slice(b * S5_WIDTH + h * LANES, b * S5_WIDTH + (h + 1) * LANES)
                ut_ref[h, pl.ds(b, steps, stride=bsz), :] = u_ref[:, chans]
                uo_ref[h, pl.ds(b, steps, stride=bsz), :] = u_old_ref[:, chans]
        u_old = jnp.concatenate([uo_ref[h] for h in range(planes)], axis=1)
        u_new = jnp.concatenate([ut_ref[h] for h in range(planes)], axis=1).astype(BF16)
        y = d_ref[...] * u_old
        slabs = S5_NSTATE // slab
        runs_per_slab = S5_INTERLEAVE // slabs
        run = steps // runs_per_slab
        kw = 2 * S5_NSTATE // S5_INTERLEAVE
        for piece in range(S5_INTERLEAVE):
            ls, part = divmod(piece, runs_per_slab)
            re = slice(ls * slab, (ls + 1) * slab)
            im = slice(S5_NSTATE + ls * slab, S5_NSTATE + (ls + 1) * slab)
            ar = jnp.broadcast_to(ar_ref[:, re], (bsz, slab))
            ai = jnp.broadcast_to(ai_ref[:, re], (bsz, slab))
            xr = st_ref[:, re]
            xi = st_ref[:, im]
            for t in range(part * run, (part + 1) * run):
                rows = slice(t * bsz, (t + 1) * bsz)
                xr, xi = (ar * xr - ai * xi + bu_ref[rows, re],
                          ar * xi + ai * xr + bu_ref[rows, im])
                x_ref[rows, re] = xr.astype(BF16)
                x_ref[rows, im] = xi.astype(BF16)
            st_ref[:, re] = xr
            st_ref[:, im] = xi

            cols = slice(piece * kw, (piece + 1) * kw)
            y = y + _dot(x_old_ref[:, cols], ct_ref[cols, :])
            bu_new_ref[:, cols] = _dot(u_new, bt_ref[:, cols])

        y = jax.nn.gelu(y)
        out = y * jax.nn.sigmoid(_dot(y.astype(BF16), wg_ref[...]))
        for h in range(planes):
            ot_ref[h] = out[:, h * LANES:(h + 1) * LANES]
        for b in range(bsz):
            for h in range(planes):
                chans = slice(b * S5_WIDTH + h * LANES, b * S5_WIDTH + (h + 1) * LANES)
                o_ref[:, chans] = ot_ref[h, pl.ds(b, steps, stride=bsz), :].astype(BF16)

    for slot in (0, 1):
        @pl.when(i % 2 == slot)
        def _():
            body(slot)


def _s5(u, bt, ar, ai, ct, d, wg, bsz, steps):
    L = u.shape[0]
    blk_rows = steps * bsz
    n_blocks = L // steps
    io_block = (steps, bsz * S5_WIDTH)
    const = lambda shape: pl.BlockSpec(shape, lambda t: (0, 0))
    return pl.pallas_call(
        functools.partial(_s5_kernel, steps=steps, bsz=bsz, slab=512),
        grid=(n_blocks + 2,),
        in_specs=[
            pl.BlockSpec(io_block, lambda t: (jnp.minimum(t, n_blocks - 1), 0)),
            pl.BlockSpec(io_block, lambda t: (jnp.maximum(t - 2, 0), 0)),
            const((S5_WIDTH, 2 * S5_NSTATE)),
            const((1, S5_NSTATE)), const((1, S5_NSTATE)),
            const((2 * S5_NSTATE, S5_WIDTH)),
            const((1, S5_WIDTH)),
            const((S5_WIDTH, S5_WIDTH)),
        ],
        out_specs=pl.BlockSpec(io_block, lambda t: (jnp.maximum(t - 2, 0), 0)),
        out_shape=jax.ShapeDtypeStruct((L, bsz * S5_WIDTH), BF16),
        scratch_shapes=[
            pltpu.VMEM((blk_rows, 2 * S5_NSTATE), F32), pltpu.VMEM((blk_rows, 2 * S5_NSTATE), F32),
            pltpu.VMEM((blk_rows, 2 * S5_NSTATE), BF16), pltpu.VMEM((blk_rows, 2 * S5_NSTATE), BF16),
            pltpu.VMEM((bsz, 2 * S5_NSTATE), F32),
            pltpu.VMEM((S5_WIDTH // LANES, blk_rows, LANES), F32),
            pltpu.VMEM((S5_WIDTH // LANES, blk_rows, LANES), F32),
            pltpu.VMEM((S5_WIDTH // LANES, blk_rows, LANES), F32),
        ],
        compiler_params=pltpu.CompilerParams(
            dimension_semantics=("arbitrary",), vmem_limit_bytes=VMEM_LIMIT_BYTES),
        name="s5_mixer",
    )(u, u, bt, ar, ai, ct, d, wg)


FF_TILE_STARTS = (0, 768, 1536, 2304, D_FF)
FINAL_ROW_TILE = 512


def _out_ffn_kernel(*refs, rows, parts, drop_prefix):
    n_act = 4 * parts
    h_refs, sb_refs, ro_refs, so_refs = (refs[i * parts:(i + 1) * parts] for i in range(4))
    wo_ref, g2_ref, wg_ref, wu_ref, wd_ref, o_ref = refs[n_act:]
    stack = lambda chunks: chunks[0] if parts == 1 else jnp.concatenate(chunks, axis=0)
    mix = jnp.concatenate([stack([r[0] for r in sb_refs]), stack([r[0] for r in ro_refs]),
                           stack([r[...] for r in so_refs])], axis=1)
    h1 = stack([r[0] for r in h_refs]) + _dot(mix, wo_ref[...])
    ms = jnp.mean(h1 * h1, axis=-1, keepdims=True)
    hn = (h1 * lax.rsqrt(ms + RMS_EPS) * g2_ref[...]).astype(BF16)
    ffn = jnp.zeros_like(h1)
    for lo, hi in zip(FF_TILE_STARTS[:-1], FF_TILE_STARTS[1:]):
        gate = _dot(hn, wg_ref[:, lo:hi])
        up = _dot(hn, wu_ref[:, lo:hi])
        act = (gate * jax.nn.sigmoid(gate)) * up
        ffn = ffn + _dot(act.astype(BF16), wd_ref[lo:hi, :])
    h2 = h1 + ffn
    if drop_prefix:
        o_ref[0] = h2
    else:
        pos = pl.program_id(1) * rows + lax.broadcasted_iota(jnp.int32, (rows, 1), 0)
        o_ref[0] = jnp.where(pos >= PAD_FRONT, h2, 0.0)


def _out_ffn(h, sb_o, ro, so_tm, w_out, g2, w_gate, w_up, w_down, tm, drop_prefix=False):
    bsz, L, _ = h.shape
    if drop_prefix:
        rows, parts, first_chunk = FINAL_ROW_TILE, FINAL_ROW_TILE // CHUNK, 1
        out_rows = L - CHUNK
    else:
        rows, parts, first_chunk = tm, 1, 0
        out_rows = L
    chunk = rows // parts
    act3 = lambda w: [pl.BlockSpec((1, chunk, w), lambda b, t, j=j: (b, first_chunk + parts * t + j, 0))
                      for j in range(parts)]
    so_specs = [pl.BlockSpec((chunk, S5_WIDTH), lambda b, t, j=j: (first_chunk + parts * t + j, b))
                for j in range(parts)]
    const = lambda shape: pl.BlockSpec(shape, lambda b, t: (0, 0), pipeline_mode=pl.Buffered(1))
    return pl.pallas_call(
        functools.partial(_out_ffn_kernel, rows=rows, parts=parts, drop_prefix=drop_prefix),
        grid=(bsz, out_rows // rows),
        in_specs=[
            *act3(D_MODEL), *act3(SB_WIDTH), *act3(RET_WIDTH), *so_specs,
            const((D_MODEL, D_MODEL)), const((1, D_MODEL)),
            const((D_MODEL, D_FF)), const((D_MODEL, D_FF)), const((D_FF, D_MODEL)),
        ],
        out_specs=pl.BlockSpec((1, rows, D_MODEL), lambda b, t: (b, t, 0)),
        out_shape=jax.ShapeDtypeStruct((bsz, out_rows, D_MODEL), F32),
        compiler_params=pltpu.CompilerParams(
            dimension_semantics=("arbitrary", "arbitrary"), vmem_limit_bytes=VMEM_LIMIT_BYTES),
        name="out_ffn",
    )(*([h] * parts), *([sb_o] * parts), *([ro] * parts), *([so_tm] * parts),
      w_out, g2, w_gate, w_up, w_down)


def _block_diag_mean(width):
    idx = jnp.arange(width) // HEAD_DIM
    return jnp.where(idx[:, None] == idx[None, :], 1.0 / HEAD_DIM, 0.0).astype(BF16)


def _suffix_count_matrix(tk):
    idx = jnp.arange(2 * tk)
    same = (idx[:, None] // tk) == (idx[None, :] // tk)
    return jnp.where(same & (idx[:, None] >= idx[None, :]), 1.0, 0.0).astype(BF16)


def _rope_tables(L):
    half = HEAD_DIM // 2
    pos = (jnp.arange(L) - PAD_FRONT).astype(F32)
    inv = ROPE_BASE ** (-jnp.arange(half, dtype=F32) / half)
    ang = pos[:, None] * inv[None, :]
    cos = jnp.cos(ang)
    sin = jnp.sin(ang)
    cos_t = jnp.tile(jnp.concatenate([cos, cos], axis=1), (1, LANES // HEAD_DIM))
    sin_t = jnp.tile(jnp.concatenate([-sin, sin], axis=1), (1, LANES // HEAD_DIM))
    return cos_t, sin_t


def _retention_tables():
    log_gamma = jnp.log1p(-jnp.exp2(-5.0 - jnp.arange(RET_HEADS, dtype=F32)))
    i = jnp.arange(CHUNK, dtype=F32)
    diff = i[:, None] - i[None, :]
    decay_in = jnp.where(diff >= 0, jnp.exp(log_gamma[:, None, None] * jnp.maximum(diff, 0.0)), 0.0)
    q_decay = jnp.exp(log_gamma[:, None] * (i + 1.0))
    k_decay = jnp.exp(log_gamma[:, None] * (CHUNK - 1.0 - i))
    chunk_decay = jnp.exp(log_gamma * CHUNK)
    per_lane = lambda t: jnp.repeat(t.T.reshape(CHUNK, HEAD_PAIRS, 2), HEAD_DIM, axis=2).transpose(1, 0, 2)
    dq = per_lane(q_decay)
    dk = per_lane(k_decay)
    din = decay_in.reshape(HEAD_PAIRS, 2, CHUNK, CHUNK).transpose(0, 2, 1, 3).reshape(HEAD_PAIRS, CHUNK, 2 * CHUNK)
    cd = jnp.repeat(chunk_decay.reshape(HEAD_PAIRS, 1, 2), HEAD_DIM, axis=2)
    return dq, dk, din, cd


def _s5_params(lam_re, lam_im, log_dt, b_re, b_im, c_re, c_im):
    dt = jnp.exp(log_dt)[:, None]
    mag = jnp.exp(lam_re * dt)
    ar = mag * jnp.cos(lam_im * dt)
    ai = mag * jnp.sin(lam_im * dt)
    den = lam_re * lam_re + lam_im * lam_im
    fr = ((ar - 1.0) * lam_re + ai * lam_im) / den
    fi = (ai * lam_re - (ar - 1.0) * lam_im) / den
    bbr = fr[..., None] * b_re - fi[..., None] * b_im
    bbi = fr[..., None] * b_im + fi[..., None] * b_re
    eye = jnp.eye(S5_GROUPS, dtype=F32)
    expand_b = lambda m: jnp.einsum('gpc,gh->gchp', m, eye).reshape(S5_WIDTH, S5_NSTATE)
    bt = jnp.concatenate([expand_b(bbr), expand_b(bbi)], axis=1).astype(BF16)
    expand_c = lambda m: jnp.einsum('gcp,gh->hpgc', m, eye).reshape(S5_NSTATE, S5_WIDTH)
    ct = jnp.concatenate([expand_c(c_re), -expand_c(c_im)], axis=0).astype(BF16)
    return bt, ar.reshape(1, S5_NSTATE), ai.reshape(1, S5_NSTATE), ct


def _largest_tile(L, candidates):
    for c in candidates:
        if L % c == 0:
            return c
    raise ValueError(f"no supported tile divides sequence length {L}")


def kernel(x, meta_tokens, norm1_g, w_in, sb_q_g, sb_k_g, ret_q_g, ret_k_g, ret_out_g, s5_lam_re, s5_lam_im,
           s5_log_dt, s5_b_re, s5_b_im, s5_c_re, s5_c_im, s5_d, s5_w_glu, w_out, norm2_g, w_gate, w_up, w_down):
    bsz, seq, _ = x.shape
    depth = w_in.shape[0]
    L = PAD_FRONT + N_META + seq
    assert L % CHUNK == 0
    tm = _largest_tile(L, (528, 384, 128))
    tq = _largest_tile(L, (384,))
    s5_steps = _largest_tile(L, (32, 16, 8))

    prefix = jnp.concatenate([jnp.zeros((PAD_FRONT, D_MODEL), x.dtype), meta_tokens.astype(x.dtype)], axis=0)
    h = x

    bd128 = _block_diag_mean(LANES)
    bd256 = _block_diag_mean(2 * LANES)
    tri = _suffix_count_matrix(CHUNK)
    cos_t, sin_t = _rope_tables(L)
    dq, dk, din, cd = _retention_tables()
    tile6 = lambda g: jnp.tile(g, SB_HEADS)

    for l in range(depth):
        head_gains = jnp.stack([tile6(sb_q_g[l]), tile6(sb_k_g[l]), tile6(ret_q_g[l]), tile6(ret_k_g[l])])
        outs = _in_proj(h, norm1_g[l][None], w_in[l].astype(BF16), head_gains, bd256, cos_t, sin_t, tm,
                        prefix=prefix if l == 0 else None)
        if l == 0:
            h = outs[-1]
        sq, sk, sv, rq, rk, rv, rg, u = outs[:8]
        sb_o = _sb_attention(sq, sk, sv, tri, tq, CHUNK)
        ro = _retention(rq, rk, rv, rg, dq, dk, din, cd, ret_out_g[l][None], bd128, tq // CHUNK)
        bt, ar, ai, ct = _s5_params(s5_lam_re[l], s5_lam_im[l], s5_log_dt[l], s5_b_re[l], s5_b_im[l],
                                    s5_c_re[l], s5_c_im[l])
        so = _s5(u, bt, ar, ai, ct, s5_d[l][None], s5_w_glu[l].astype(BF16), bsz, s5_steps)
        h = _out_ffn(h, sb_o, ro, so, w_out[l].astype(BF16), norm2_g[l][None],
                     w_gate[l].astype(BF16), w_up[l].astype(BF16), w_down[l].astype(BF16), tm,
                     drop_prefix=(l == depth - 1 and seq % FINAL_ROW_TILE == 0))
    return h if h.shape[1] == seq else h[:, PAD_FRONT + N_META:]
```

```python
import functools
import math

import jax
import jax.numpy as jnp
from jax import lax
from jax.experimental import pallas as pl
from jax.experimental.pallas import tpu as pltpu

D_MODEL = 1024
HEAD_DIM = 64
SB_HEADS = 6
RET_HEADS = 6
SB_WIDTH = SB_HEADS * HEAD_DIM
RET_WIDTH = RET_HEADS * HEAD_DIM
S5_WIDTH = D_MODEL - SB_WIDTH - RET_WIDTH
S5_GROUP_CH = 16
S5_GROUPS = S5_WIDTH // S5_GROUP_CH
S5_STATE = 64
S5_NSTATE = S5_GROUPS * S5_STATE
D_IN = 3 * SB_WIDTH + 4 * RET_WIDTH + S5_WIDTH
D_FF = 2816
CHUNK = 128
N_META = 16
PAD_FRONT = CHUNK - N_META
ROPE_BASE = 10000.0
RMS_EPS = 1e-6
SB_QUERY_SCALE = HEAD_DIM ** -0.5 * math.log2(math.e)
S5_INTERLEAVE = 4
MASKED_LOG2_WEIGHT = -1e30

LANES = 128
HEAD_PAIRS = SB_HEADS // 2
VMEM_LIMIT_BYTES = 56 * 1024 * 1024

F32 = jnp.float32
BF16 = jnp.bfloat16


def _dot(a, b):
    return jnp.dot(a, b, preferred_element_type=F32)


def _dot_nt(a, b):
    return lax.dot_general(a, b, (((1,), (1,)), ((), ())), preferred_element_type=F32)


def _dot_tn(a, b):
    return lax.dot_general(a, b, (((0,), (0,)), ((), ())), preferred_element_type=F32)


def _split_dot(x, m):
    hi = x.astype(BF16)
    lo = (x - hi.astype(F32)).astype(BF16)
    return _dot(hi, m) + _dot(lo, m)


def _head_split(t, first_head):
    zero = jnp.zeros_like(t)
    return jnp.concatenate([jnp.where(first_head, t, zero), jnp.where(first_head, zero, t)], axis=0)


def _first_head_lanes():
    return lax.broadcasted_iota(jnp.int32, (1, LANES), 1) < HEAD_DIM


TOKEN_TILE = 3 * CHUNK


def _in_kernel(*refs, from_tokens):
    if from_tokens:
        parts = TOKEN_TILE // CHUNK
        x_refs, prefix_ref, refs = refs[:parts], refs[parts], refs[parts + 1:]
        head = jnp.where(pl.program_id(1) == 0, prefix_ref[...], x_refs[0][0])
        x = jnp.concatenate([head] + [r[0] for r in x_refs[1:]], axis=0)
        refs[-1][0] = x
        refs = refs[:-1]
    else:
        x = refs[0][0]
        refs = refs[1:]
    (g1_ref, w_ref, hg_ref, bd_ref, cos_ref, sin_ref,
     sq_ref, sk_ref, sv_ref, rq_ref, rk_ref, rv_ref, rg_ref, u_ref) = refs
    ms = jnp.mean(x * x, axis=-1, keepdims=True)
    hn = (x * lax.rsqrt(ms + RMS_EPS) * g1_ref[...]).astype(BF16)

    def proj_pair(seg):
        p = _dot(hn, w_ref[:, seg * SB_WIDTH:min((seg + 2) * SB_WIDTH, D_IN)])
        return p[:, :SB_WIDTH], p[:, SB_WIDTH:]

    def head_norm_pair(a, b, gi):
        p = jnp.concatenate([a, b], axis=1)
        sq = (p * p).astype(BF16)
        wide = bd_ref.shape[0]
        m = jnp.concatenate(
            [_dot(sq[:, c:c + wide], bd_ref[...]) for c in range(0, 2 * SB_WIDTH, wide)], axis=1)
        gains = jnp.concatenate([hg_ref[gi:gi + 1, :], hg_ref[gi + 1:gi + 2, :]], axis=1)
        p = p * lax.rsqrt(m + RMS_EPS) * gains
        return p[:, :SB_WIDTH], p[:, SB_WIDTH:]

    lane = lax.broadcasted_iota(jnp.int32, (1, LANES), 1)
    lower = (lane & (HEAD_DIM - 1)) < HEAD_DIM // 2
    cos = cos_ref[...]
    sin = sin_ref[...]

    def rotary(t):
        outs = []
        for c in range(HEAD_PAIRS):
            tc = t[:, c * LANES:(c + 1) * LANES]
            partner = jnp.where(lower, pltpu.roll(tc, LANES - HEAD_DIM // 2, 1),
                                pltpu.roll(tc, HEAD_DIM // 2, 1))
            outs.append(tc * cos + partner * sin)
        return jnp.concatenate(outs, axis=1)

    wide_lane = lax.broadcasted_iota(jnp.int32, (1, SB_WIDTH), 1)
    first_head = (wide_lane & (LANES - 1)) < HEAD_DIM

    def store_per_head(ref, t):
        t = t.astype(BF16)
        zero = jnp.zeros_like(t)
        ref[0, 0] = jnp.where(first_head, t, zero)
        ref[0, 1] = jnp.where(first_head, zero, t)

    sq, sk = head_norm_pair(*proj_pair(0), 0)
    sq_ref[0] = (sq * SB_QUERY_SCALE).astype(BF16)
    store_per_head(sk_ref, sk)
    sv, rq = proj_pair(2)
    store_per_head(sv_ref, sv)
    rk, rv = proj_pair(4)
    rv_ref[0] = rv.astype(BF16)
    rq, rk = head_norm_pair(rq, rk, 2)
    rq_ref[0] = rotary(rq)
    rk_ref[0] = rotary(rk)
    rg_ref[0], u_ref[...] = proj_pair(6)


def _in_proj(h, g1, w_in, head_gains, bd, cos_t, sin_t, tm, prefix=None):
    from_tokens = prefix is not None
    bsz = h.shape[0]
    if from_tokens:
        tm = TOKEN_TILE
        L = h.shape[1] + CHUNK
        parts = tm // CHUNK
        act_specs = [pl.BlockSpec((1, CHUNK, D_MODEL), lambda b, t, j=j: (b, jnp.maximum(parts * t - 1 + j, 0), 0))
                     for j in range(parts)]
        act_specs.append(pl.BlockSpec((CHUNK, D_MODEL), lambda b, t: (0, 0)))
        acts = [h] * parts + [prefix]
    else:
        L = h.shape[1]
        act_specs = [pl.BlockSpec((1, tm, D_MODEL), lambda b, t: (b, t, 0))]
        acts = [h]
    grid = (bsz, L // tm)
    row3 = lambda w: pl.BlockSpec((1, tm, w), lambda b, t: (b, t, 0))
    const = lambda shape: pl.BlockSpec(shape, lambda b, t: (0,) * len(shape), pipeline_mode=pl.Buffered(1))
    out_shapes = [
        jax.ShapeDtypeStruct((bsz, L, SB_WIDTH), BF16),
        jax.ShapeDtypeStruct((bsz, 2, L, SB_WIDTH), BF16),
        jax.ShapeDtypeStruct((bsz, 2, L, SB_WIDTH), BF16),
        jax.ShapeDtypeStruct((bsz, L, RET_WIDTH), F32),
        jax.ShapeDtypeStruct((bsz, L, RET_WIDTH), F32),
        jax.ShapeDtypeStruct((bsz, L, RET_WIDTH), BF16),
        jax.ShapeDtypeStruct((bsz, L, RET_WIDTH), F32),
        jax.ShapeDtypeStruct((L, bsz * S5_WIDTH), F32),
    ]
    per_head = pl.BlockSpec((1, 2, tm, SB_WIDTH), lambda b, t: (b, 0, t, 0))
    out_specs = ([row3(SB_WIDTH), per_head, per_head] + [row3(RET_WIDTH)] * 4
                 + [pl.BlockSpec((tm, S5_WIDTH), lambda b, t: (t, b))])
    if from_tokens:
        out_shapes.append(jax.ShapeDtypeStruct((bsz, L, D_MODEL), F32))
        out_specs.append(row3(D_MODEL))
    return pl.pallas_call(
        functools.partial(_in_kernel, from_tokens=from_tokens),
        grid=grid,
        in_specs=[
            *act_specs,
            const((1, D_MODEL)),
            const((D_MODEL, D_IN)),
            const((4, SB_WIDTH)),
            const(bd.shape),
            pl.BlockSpec((tm, LANES), lambda b, t: (t, 0)),
            pl.BlockSpec((tm, LANES), lambda b, t: (t, 0)),
        ],
        out_specs=out_specs,
        out_shape=out_shapes,
        compiler_params=pltpu.CompilerParams(
            dimension_semantics=("arbitrary", "arbitrary"), vmem_limit_bytes=VMEM_LIMIT_BYTES),
        name="in_proj",
    )(*acts, g1, w_in, head_gains, bd, cos_t, sin_t)


LAST_SLOT = 2


def _sb_kernel(q_ref, k_ref, v_ref, tri_ref, o_ref, acc_ref, c_ref, accp_ref, cp_ref, sp_ref, z_ref, *, tq, tk):
    qi = pl.program_id(1)
    n_q_blocks = pl.num_programs(1) - 1
    n_key_blocks = k_ref.shape[2] // tk
    width = 2 * tk
    t_idx = qi * tq + lax.broadcasted_iota(jnp.int32, (tq, 1), 0)
    s_lane = lax.broadcasted_iota(jnp.int32, (1, width), 1) & (tk - 1)

    n_pairs = ((qi + 1) * (tq // tk) + 1) // 2

    def block_rows(j):
        return pl.ds(pl.multiple_of(jnp.minimum(j, n_key_blocks - 1) * tk, tk), tk)

    def lanes(p):
        return slice(p * LANES, (p + 1) * LANES)

    def per_head_rows(ref, j, p):
        return jnp.concatenate([ref[0, 0, block_rows(j), lanes(p)], ref[0, 1, block_rows(j), lanes(p)]], axis=0)

    def score_dots(m, p):
        q = q_ref[0, :, lanes(p)]
        return [_dot_nt(q, per_head_rows(k_ref, j, p)) for j in (2 * m + 1, 2 * m)]

    def score_logs(zs, m, p, slot, causal, padded):
        for half, (z, j) in enumerate(zip(zs, (2 * m + 1, 2 * m))):
            if causal or padded:
                s_idx = j * tk + s_lane
                mask = s_idx < t_idx if causal else s_idx >= PAD_FRONT
                if causal and padded:
                    mask = mask & (s_idx >= PAD_FRONT)
                z = jnp.where(mask, z, MASKED_LOG2_WEIGHT)
            neg_abs = lax.bitcast_convert_type(
                lax.bitcast_convert_type(z, jnp.uint32) | jnp.uint32(0x80000000), F32)
            sp = jnp.maximum(z, 0.0) + jnp.log2(1.0 + jnp.exp2(neg_abs))
            cols = slice(half * width, (half + 1) * width)
            sp_ref[p, slot, :, cols] = sp.astype(BF16)
            z_ref[p, slot, :, cols] = z

    def weights(slot, p, c_in_ref, c_out_ref):
        c = c_in_ref[p]
        ws = []
        for half in range(2):
            cols = slice(half * width, (half + 1) * width)
            later = _dot(sp_ref[p, slot, :, cols], tri_ref[...]) + c
            ws.append(jnp.exp2(z_ref[p, slot, :, cols] - later).astype(BF16))
            c = jnp.concatenate([jnp.broadcast_to(later[:, 0:1], (tq, tk)),
                                 jnp.broadcast_to(later[:, tk:tk + 1], (tq, tk))], axis=1)
        if c_out_ref is not None:
            c_out_ref[p] = c
        return jnp.concatenate(ws, axis=1)

    def weighted_values(w, m, p):
        vv = jnp.concatenate([per_head_rows(v_ref, j, p) for j in (2 * m + 1, 2 * m)], axis=0)
        return _dot(w, vv)

    def step(m, slot, causal, padded, last=False):
        weight_slot = 1 if last else 1 - slot
        c_out_ref, acc_out_ref = (cp_ref, accp_ref) if last else (c_ref, acc_ref)
        for p in range(HEAD_PAIRS):
            score_logs(score_dots(m, p), m, p, slot, causal, padded)
            w = weights(weight_slot, p, c_ref, c_out_ref)
            acc_out_ref[:, lanes(p)] = acc_ref[:, lanes(p)] + weighted_values(w, m + 1, p)

    def finish_previous(p):
        w = weights(LAST_SLOT, p, cp_ref, None)
        o_ref[0, :, lanes(p)] = (accp_ref[:, lanes(p)] + weighted_values(w, 0, p)).astype(BF16)

    @pl.when(qi == 0)
    def _():
        acc_ref[...] = jnp.zeros_like(acc_ref)
        c_ref[...] = jnp.zeros_like(c_ref)
        for p in range(HEAD_PAIRS):
            score_logs(score_dots(1, p), 1, p, 1, True, False)
        step(0, LAST_SLOT, True, True, last=True)

    @pl.when(qi == n_q_blocks)
    def _():
        for p in range(HEAD_PAIRS):
            finish_previous(p)

    @pl.when(jnp.logical_and(qi > 0, qi < n_q_blocks))
    def _():
        for parity in (0, 1):
            @pl.when((n_pairs - 1) % 2 == parity)
            def _():
                acc_ref[...] = jnp.zeros_like(acc_ref)
                c_ref[...] = jnp.zeros_like(c_ref)
                for p in range(HEAD_PAIRS):
                    finish_previous(p)
                    score_logs(score_dots(n_pairs - 1, p), n_pairs - 1, p, parity, True, False)
                step(n_pairs - 2, 1 - parity, True, False)

        @pl.when(n_pairs % 2 == 0)
        def _():
            step(n_pairs - 3, 1, False, False)

        n_double = (n_pairs - 3) // 2

        def body(i, carry):
            m = 2 * (n_double - i)
            step(m, 0, False, False)
            step(m - 1, 1, False, False)
            return carry

        lax.fori_loop(0, n_double, body, 0)
        step(0, LAST_SLOT, False, True, last=True)


def _sb_attention(q, k, v, tri, tq, tk):
    bsz, L, _ = q.shape
    assert tq % tk == 0 and tq // tk >= 3, "the pipeline prologue needs two key-block pairs per query block"
    n_q_blocks = L // tq
    slot_shape = (HEAD_PAIRS, 3, tq, 4 * tk)
    return pl.pallas_call(
        functools.partial(_sb_kernel, tq=tq, tk=tk),
        grid=(bsz, n_q_blocks + 1),
        in_specs=[
            pl.BlockSpec((1, tq, SB_WIDTH), lambda b, i: (b, jnp.minimum(i, n_q_blocks - 1), 0)),
            pl.BlockSpec((1, 2, L, SB_WIDTH), lambda b, i: (b, 0, 0, 0)),
            pl.BlockSpec((1, 2, L, SB_WIDTH), lambda b, i: (b, 0, 0, 0)),
            pl.BlockSpec((2 * tk, 2 * tk), lambda b, i: (0, 0)),
        ],
        out_specs=pl.BlockSpec((1, tq, SB_WIDTH), lambda b, i: (b, jnp.maximum(i - 1, 0), 0)),
        out_shape=jax.ShapeDtypeStruct((bsz, L, SB_WIDTH), BF16),
        scratch_shapes=[pltpu.VMEM((tq, SB_WIDTH), F32), pltpu.VMEM((HEAD_PAIRS, tq, 2 * tk), F32),
                        pltpu.VMEM((tq, SB_WIDTH), F32), pltpu.VMEM((HEAD_PAIRS, tq, 2 * tk), F32),
                        pltpu.VMEM(slot_shape, BF16), pltpu.VMEM(slot_shape, F32)],
        compiler_params=pltpu.CompilerParams(
            dimension_semantics=("arbitrary", "arbitrary"), vmem_limit_bytes=VMEM_LIMIT_BYTES),
        name="sb_attention",
    )(q, k, v, tri)


def _ret_kernel(q_ref, k_ref, v_ref, g_ref, dq_ref, dk_ref, din_ref, cd_ref, go_ref, bd_ref,
                o_ref, s_ref, *, chunks):
    @pl.when(pl.program_id(1) == 0)
    def _():
        s_ref[...] = jnp.zeros_like(s_ref)

    first = _first_head_lanes()
    same_head = (lax.broadcasted_iota(jnp.int32, (LANES, 1), 0) < HEAD_DIM) == first
    units = [(c, p) for c in range(chunks) for p in range(HEAD_PAIRS)]

    def block(ref, c, p):
        return ref[0, c * CHUNK:(c + 1) * CHUNK, p * LANES:(p + 1) * LANES]

    q = {u: block(q_ref, *u) for u in units}
    k = {u: block(k_ref, *u) * (HEAD_DIM ** -0.5) for u in units}
    v = {u: block(v_ref, *u) for u in units}
    scores = {(c, p): _dot_nt(q[c, p].astype(BF16), _head_split(k[c, p].astype(BF16), first)) * din_ref[p]
              for c, p in units}
    inner = {u: _dot(scores[u].astype(BF16), _head_split(v[u], first)) for u in units}
    upd = {(c, p): _dot_tn((k[c, p] * dk_ref[p]).astype(BF16), v[c, p]) for c, p in units}
    out = {}
    for p in range(HEAD_PAIRS):
        state = s_ref[p]
        for c in range(chunks):
            cross = _dot((q[c, p] * dq_ref[p]).astype(BF16), state.astype(BF16))
            state = state * cd_ref[p] + jnp.where(same_head, upd[c, p], 0.0)
            out[c, p] = inner[c, p] + cross
        s_ref[p] = state
    for c, p in units:
        o = out[c, p]
        ms = _split_dot(o * o, bd_ref[...])
        on = o * lax.rsqrt(ms + RMS_EPS) * go_ref[:, p * LANES:(p + 1) * LANES]
        g = block(g_ref, c, p)
        o_ref[0, c * CHUNK:(c + 1) * CHUNK, p * LANES:(p + 1) * LANES] = (
            on * (g * jax.nn.sigmoid(g))).astype(BF16)


def _retention(q, k, v, g, dq, dk, din, cd, gout, bd, chunks):
    bsz, L, _ = q.shape
    rows = chunks * CHUNK
    grid = (bsz, L // rows)
    blk = pl.BlockSpec((1, rows, RET_WIDTH), lambda b, n: (b, n, 0))
    const = lambda shape: pl.BlockSpec(shape, lambda b, n: (0,) * len(shape))
    return pl.pallas_call(
        functools.partial(_ret_kernel, chunks=chunks),
        grid=grid,
        in_specs=[
            blk, blk, blk, blk,
            const((HEAD_PAIRS, CHUNK, LANES)), const((HEAD_PAIRS, CHUNK, LANES)),
            const((HEAD_PAIRS, CHUNK, 2 * CHUNK)), const((HEAD_PAIRS, 1, LANES)),
            const((1, RET_WIDTH)), const((LANES, LANES)),
        ],
        out_specs=blk,
        out_shape=jax.ShapeDtypeStruct((bsz, L, RET_WIDTH), BF16),
        scratch_shapes=[pltpu.VMEM((HEAD_PAIRS, LANES, LANES), F32)],
        compiler_params=pltpu.CompilerParams(dimension_semantics=("arbitrary", "arbitrary")),
        name="retention",
    )(q, k, v, g, dq, dk, din, cd, gout, bd)


def _s5_kernel(u_ref, u_old_ref, bt_ref, ar_ref, ai_ref, ct_ref, d_ref, wg_ref, o_ref,
               bu0_ref, bu1_ref, x0_ref, x1_ref, st_ref, ut_ref, uo_ref, ot_ref, *, steps, bsz, slab):
    i = pl.program_id(0)
    bu_refs = (bu0_ref, bu1_ref)
    x_refs = (x0_ref, x1_ref)

    @pl.when(i == 0)
    def _():
        for ref in (st_ref, bu0_ref, bu1_ref, x0_ref, x1_ref):
            ref[...] = jnp.zeros_like(ref)

    def body(slot):
        bu_new_ref, bu_ref = bu_refs[slot], bu_refs[1 - slot]
        x_old_ref, x_ref = x_refs[slot], x_refs[1 - slot]
        planes = S5_WIDTH // LANES
        for b in range(bsz):
            for h in range(planes):
                chans = slice(b * S5_WIDTH + h * LANES, b * S5_WIDTH + (h + 1) * LANES)
                ut_ref[h, pl.ds(b, steps, stride=bsz), :] = u_ref[:, chans]
                uo_ref[h, pl.ds(b, steps, stride=bsz), :] = u_old_ref[:, chans]
        u_old = jnp.concatenate([uo_ref[h] for h in range(planes)], axis=1)
        u_new = jnp.concatenate([ut_ref[h] for h in range(planes)], axis=1).astype(BF16)
        y = d_ref[...] * u_old
        slabs = S5_NSTATE // slab
        runs_per_slab = S5_INTERLEAVE // slabs
        run = steps // runs_per_slab
        kw = 2 * S5_NSTATE // S5_INTERLEAVE
        for piece in range(S5_INTERLEAVE):
            ls, part = divmod(piece, runs_per_slab)
            re = slice(ls * slab, (ls + 1) * slab)
            im = slice(S5_NSTATE + ls * slab, S5_NSTATE + (ls + 1) * slab)
            ar = jnp.broadcast_to(ar_ref[:, re], (bsz, slab))
            ai = jnp.broadcast_to(ai_ref[:, re], (bsz, slab))
            xr = st_ref[:, re]
            xi = st_ref[:, im]
            for t in range(part * run, (part + 1) * run):
                rows = slice(t * bsz, (t + 1) * bsz)
                xr, xi = (ar * xr - ai * xi + bu_ref[rows, re],
                          ar * xi + ai * xr + bu_ref[rows, im])
                x_ref[rows, re] = xr.astype(BF16)
                x_ref[rows, im] = xi.astype(BF16)
            st_ref[:, re] = xr
            st_ref[:, im] = xi

            cols = slice(piece * kw, (piece + 1) * kw)
            y = y + _dot(x_old_ref[:, cols], ct_ref[cols, :])
            bu_new_ref[:, cols] = _dot(u_new, bt_ref[:, cols])

        y = jax.nn.gelu(y)
        out = y * jax.nn.sigmoid(_dot(y.astype(BF16), wg_ref[...]))
        for h in range(planes):
            ot_ref[h] = out[:, h * LANES:(h + 1) * LANES]
        for b in range(bsz):
            for h in range(planes):
                chans = slice(b * S5_WIDTH + h * LANES, b * S5_WIDTH + (h + 1) * LANES)
                o_ref[:, chans] = ot_ref[h, pl.ds(b, steps, stride=bsz), :].astype(BF16)

    for slot in (0, 1):
        @pl.when(i % 2 == slot)
        def _():
            body(slot)


def _s5(u, bt, ar, ai, ct, d, wg, bsz, steps):
    L = u.shape[0]
    blk_rows = steps * bsz
    n_blocks = L // steps
    io_block = (steps, bsz * S5_WIDTH)
    const = lambda shape: pl.BlockSpec(shape, lambda t: (0, 0))
    return pl.pallas_call(
        functools.partial(_s5_kernel, steps=steps, bsz=bsz, slab=512),
        grid=(n_blocks + 2,),
        in_specs=[
            pl.BlockSpec(io_block, lambda t: (jnp.minimum(t, n_blocks - 1), 0)),
            pl.BlockSpec(io_block, lambda t: (jnp.maximum(t - 2, 0), 0)),
            const((S5_WIDTH, 2 * S5_NSTATE)),
            const((1, S5_NSTATE)), const((1, S5_NSTATE)),
            const((2 * S5_NSTATE, S5_WIDTH)),
            const((1, S5_WIDTH)),
            const((S5_WIDTH, S5_WIDTH)),
        ],
        out_specs=pl.BlockSpec(io_block, lambda t: (jnp.maximum(t - 2, 0), 0)),
        out_shape=jax.ShapeDtypeStruct((L, bsz * S5_WIDTH), BF16),
        scratch_shapes=[
            pltpu.VMEM((blk_rows, 2 * S5_NSTATE), F32), pltpu.VMEM((blk_rows, 2 * S5_NSTATE), F32),
            pltpu.VMEM((blk_rows, 2 * S5_NSTATE), BF16), pltpu.VMEM((blk_rows, 2 * S5_NSTATE), BF16),
            pltpu.VMEM((bsz, 2 * S5_NSTATE), F32),
            pltpu.VMEM((S5_WIDTH // LANES, blk_rows, LANES), F32),
            pltpu.VMEM((S5_WIDTH // LANES, blk_rows, LANES), F32),
            pltpu.VMEM((S5_WIDTH // LANES, blk_rows, LANES), F32),
        ],
        compiler_params=pltpu.CompilerParams(
            dimension_semantics=("arbitrary",), vmem_limit_bytes=VMEM_LIMIT_BYTES),
        name="s5_mixer",
    )(u, u, bt, ar, ai, ct, d, wg)


FF_TILE_STARTS = (0, 768, 1536, 2304, D_FF)
FINAL_ROW_TILE = 512


def _out_ffn_kernel(*refs, rows, parts, drop_prefix):
    n_act = 4 * parts
    h_refs, sb_refs, ro_refs, so_refs = (refs[i * parts:(i + 1) * parts] for i in range(4))
    wo_ref, g2_ref, wg_ref, wu_ref, wd_ref, o_ref = refs[n_act:]
    stack = lambda chunks: chunks[0] if parts == 1 else jnp.concatenate(chunks, axis=0)
    mix = jnp.concatenate([stack([r[0] for r in sb_refs]), stack([r[0] for r in ro_refs]),
                           stack([r[...] for r in so_refs])], axis=1)
    h1 = stack([r[0] for r in h_refs]) + _dot(mix, wo_ref[...])
    ms = jnp.mean(h1 * h1, axis=-1, keepdims=True)
    hn = (h1 * lax.rsqrt(ms + RMS_EPS) * g2_ref[...]).astype(BF16)
    ffn = jnp.zeros_like(h1)
    for lo, hi in zip(FF_TILE_STARTS[:-1], FF_TILE_STARTS[1:]):
        gate = _dot(hn, wg_ref[:, lo:hi])
        up = _dot(hn, wu_ref[:, lo:hi])
        act = (gate * jax.nn.sigmoid(gate)) * up
        ffn = ffn + _dot(act.astype(BF16), wd_ref[lo:hi, :])
    h2 = h1 + ffn
    if drop_prefix:
        o_ref[0] = h2
    else:
        pos = pl.program_id(1) * rows + lax.broadcasted_iota(jnp.int32, (rows, 1), 0)
        o_ref[0] = jnp.where(pos >= PAD_FRONT, h2, 0.0)


def _out_ffn(h, sb_o, ro, so_tm, w_out, g2, w_gate, w_up, w_down, tm, drop_prefix=False):
    bsz, L, _ = h.shape
    if drop_prefix:
        rows, parts, first_chunk = FINAL_ROW_TILE, FINAL_ROW_TILE // CHUNK, 1
        out_rows = L - CHUNK
    else:
        rows, parts, first_chunk = tm, 1, 0
        out_rows = L
    chunk = rows // parts
    act3 = lambda w: [pl.BlockSpec((1, chunk, w), lambda b, t, j=j: (b, first_chunk + parts * t + j, 0))
                      for j in range(parts)]
    so_specs = [pl.BlockSpec((chunk, S5_WIDTH), lambda b, t, j=j: (first_chunk + parts * t + j, b))
                for j in range(parts)]
    const = lambda shape: pl.BlockSpec(shape, lambda b, t: (0, 0), pipeline_mode=pl.Buffered(1))
    return pl.pallas_call(
        functools.partial(_out_ffn_kernel, rows=rows, parts=parts, drop_prefix=drop_prefix),
        grid=(bsz, out_rows // rows),
        in_specs=[
            *act3(D_MODEL), *act3(SB_WIDTH), *act3(RET_WIDTH), *so_specs,
            const((D_MODEL, D_MODEL)), const((1, D_MODEL)),
            const((D_MODEL, D_FF)), const((D_MODEL, D_FF)), const((D_FF, D_MODEL)),
        ],
        out_specs=pl.BlockSpec((1, rows, D_MODEL), lambda b, t: (b, t, 0)),
        out_shape=jax.ShapeDtypeStruct((bsz, out_rows, D_MODEL), F32),
        compiler_params=pltpu.CompilerParams(
            dimension_semantics=("arbitrary", "arbitrary"), vmem_limit_bytes=VMEM_LIMIT_BYTES),
        name="out_ffn",
    )(*([h] * parts), *([sb_o] * parts), *([ro] * parts), *([so_tm] * parts),
      w_out, g2, w_gate, w_up, w_down)


def _block_diag_mean(width):
    idx = jnp.arange(width) // HEAD_DIM
    return jnp.where(idx[:, None] == idx[None, :], 1.0 / HEAD_DIM, 0.0).astype(BF16)


def _suffix_count_matrix(tk):
    idx = jnp.arange(2 * tk)
    same = (idx[:, None] // tk) == (idx[None, :] // tk)
    return jnp.where(same & (idx[:, None] >= idx[None, :]), 1.0, 0.0).astype(BF16)


def _rope_tables(L):
    half = HEAD_DIM // 2
    pos = (jnp.arange(L) - PAD_FRONT).astype(F32)
    inv = ROPE_BASE ** (-jnp.arange(half, dtype=F32) / half)
    ang = pos[:, None] * inv[None, :]
    cos = jnp.cos(ang)
    sin = jnp.sin(ang)
    cos_t = jnp.tile(jnp.concatenate([cos, cos], axis=1), (1, LANES // HEAD_DIM))
    sin_t = jnp.tile(jnp.concatenate([-sin, sin], axis=1), (1, LANES // HEAD_DIM))
    return cos_t, sin_t


def _retention_tables():
    log_gamma = jnp.log1p(-jnp.exp2(-5.0 - jnp.arange(RET_HEADS, dtype=F32)))
    i = jnp.arange(CHUNK, dtype=F32)
    diff = i[:, None] - i[None, :]
    decay_in = jnp.where(diff >= 0, jnp.exp(log_gamma[:, None, None] * jnp.maximum(diff, 0.0)), 0.0)
    q_decay = jnp.exp(log_gamma[:, None] * (i + 1.0))
    k_decay = jnp.exp(log_gamma[:, None] * (CHUNK - 1.0 - i))
    chunk_decay = jnp.exp(log_gamma * CHUNK)
    per_lane = lambda t: jnp.repeat(t.T.reshape(CHUNK, HEAD_PAIRS, 2), HEAD_DIM, axis=2).transpose(1, 0, 2)
    dq = per_lane(q_decay)
    dk = per_lane(k_decay)
    din = decay_in.reshape(HEAD_PAIRS, 2, CHUNK, CHUNK).transpose(0, 2, 1, 3).reshape(HEAD_PAIRS, CHUNK, 2 * CHUNK)
    cd = jnp.repeat(chunk_decay.reshape(HEAD_PAIRS, 1, 2), HEAD_DIM, axis=2)
    return dq, dk, din, cd


def _s5_params(lam_re, lam_im, log_dt, b_re, b_im, c_re, c_im):
    dt = jnp.exp(log_dt)[:, None]
    mag = jnp.exp(lam_re * dt)
    ar = mag * jnp.cos(lam_im * dt)
    ai = mag * jnp.sin(lam_im * dt)
    den = lam_re * lam_re + lam_im * lam_im
    fr = ((ar - 1.0) * lam_re + ai * lam_im) / den
    fi = (ai * lam_re - (ar - 1.0) * lam_im) / den
    bbr = fr[..., None] * b_re - fi[..., None] * b_im
    bbi = fr[..., None] * b_im + fi[..., None] * b_re
    eye = jnp.eye(S5_GROUPS, dtype=F32)
    expand_b = lambda m: jnp.einsum('gpc,gh->gchp', m, eye).reshape(S5_WIDTH, S5_NSTATE)
    bt = jnp.concatenate([expand_b(bbr), expand_b(bbi)], axis=1).astype(BF16)
    expand_c = lambda m: jnp.einsum('gcp,gh->hpgc', m, eye).reshape(S5_NSTATE, S5_WIDTH)
    ct = jnp.concatenate([expand_c(c_re), -expand_c(c_im)], axis=0).astype(BF16)
    return bt, ar.reshape(1, S5_NSTATE), ai.reshape(1, S5_NSTATE), ct


def _largest_tile(L, candidates):
    for c in candidates:
        if L % c == 0:
            return c
    raise ValueError(f"no supported tile divides sequence length {L}")


def kernel(x, meta_tokens, norm1_g, w_in, sb_q_g, sb_k_g, ret_q_g, ret_k_g, ret_out_g, s5_lam_re, s5_lam_im,
           s5_log_dt, s5_b_re, s5_b_im, s5_c_re, s5_c_im, s5_d, s5_w_glu, w_out, norm2_g, w_gate, w_up, w_down):
    bsz, seq, _ = x.shape
    depth = w_in.shape[0]
    L = PAD_FRONT + N_META + seq
    assert L % CHUNK == 0
    tm = _largest_tile(L, (528, 384, 128))
    tq = _largest_tile(L, (384,))
    s5_steps = _largest_tile(L, (32, 16, 8))

    prefix = jnp.concatenate([jnp.zeros((PAD_FRONT, D_MODEL), x.dtype), meta_tokens.astype(x.dtype)], axis=0)
    h = x

    bd128 = _block_diag_mean(LANES)
    bd256 = _block_diag_mean(2 * LANES)
    tri = _suffix_count_matrix(CHUNK)
    cos_t, sin_t = _rope_tables(L)
    dq, dk, din, cd = _retention_tables()
    tile6 = lambda g: jnp.tile(g, SB_HEADS)

    for l in range(depth):
        head_gains = jnp.stack([tile6(sb_q_g[l]), tile6(sb_k_g[l]), tile6(ret_q_g[l]), tile6(ret_k_g[l])])
        outs = _in_proj(h, norm1_g[l][None], w_in[l].astype(BF16), head_gains, bd256, cos_t, sin_t, tm,
                        prefix=prefix if l == 0 else None)
        if l == 0:
            h = outs[-1]
        sq, sk, sv, rq, rk, rv, rg, u = outs[:8]
        sb_o = _sb_attention(sq, sk, sv, tri, tq, CHUNK)
        ro = _retention(rq, rk, rv, rg, dq, dk, din, cd, ret_out_g[l][None], bd128, tq // CHUNK)
        bt, ar, ai, ct = _s5_params(s5_lam_re[l], s5_lam_im[l], s5_log_dt[l], s5_b_re[l], s5_b_im[l],
                                    s5_c_re[l], s5_c_im[l])
        so = _s5(u, bt, ar, ai, ct, s5_d[l][None], s5_w_glu[l].astype(BF16), bsz, s5_steps)
        h = _out_ffn(h, sb_o, ro, so, w_out[l].astype(BF16), norm2_g[l][None],
                     w_gate[l].astype(BF16), w_up[l].astype(BF16), w_down[l].astype(BF16), tm,
                     drop_prefix=(l == depth - 1 and seq % FINAL_ROW_TILE == 0))
    return h if h.shape[1] == seq else h[:, PAD_FRONT + N_META:]
```

```python
import functools
import math

import jax
import jax.numpy as jnp
from jax import lax
from jax.experimental import pallas as pl
from jax.experimental.pallas import tpu as pltpu

D_MODEL = 1024
HEAD_DIM = 64
SB_HEADS = 6
RET_HEADS = 6
SB_WIDTH = SB_HEADS * HEAD_DIM
RET_WIDTH = RET_HEADS * HEAD_DIM
S5_WIDTH = D_MODEL - SB_WIDTH - RET_WIDTH
S5_GROUP_CH = 16
S5_GROUPS = S5_WIDTH // S5_GROUP_CH
S5_STATE = 64
S5_NSTATE = S5_GROUPS * S5_STATE
D_IN = 3 * SB_WIDTH + 4 * RET_WIDTH + S5_WIDTH
D_FF = 2816
CHUNK = 128
N_META = 16
PAD_FRONT = CHUNK - N_META
ROPE_BASE = 10000.0
RMS_EPS = 1e-6
SB_QUERY_SCALE = HEAD_DIM ** -0.5 * math.log2(math.e)
S5_INTERLEAVE = 4
MASKED_LOG2_WEIGHT = -1e30

LANES = 128
HEAD_PAIRS = SB_HEADS // 2
VMEM_LIMIT_BYTES = 56 * 1024 * 1024

F32 = jnp.float32
BF16 = jnp.bfloat16


def _dot(a, b):
    return jnp.dot(a, b, preferred_element_type=F32)


def _dot_nt(a, b):
    return lax.dot_general(a, b, (((1,), (1,)), ((), ())), preferred_element_type=F32)


def _dot_tn(a, b):
    return lax.dot_general(a, b, (((0,), (0,)), ((), ())), preferred_element_type=F32)


def _split_dot(x, m):
    hi = x.astype(BF16)
    lo = (x - hi.astype(F32)).astype(BF16)
    return _dot(hi, m) + _dot(lo, m)


def _head_split(t, first_head):
    zero = jnp.zeros_like(t)
    return jnp.concatenate([jnp.where(first_head, t, zero), jnp.where(first_head, zero, t)], axis=0)


def _first_head_lanes():
    return lax.broadcasted_iota(jnp.int32, (1, LANES), 1) < HEAD_DIM


TOKEN_TILE = 3 * CHUNK


def _in_kernel(*refs, from_tokens):
    if from_tokens:
        parts = TOKEN_TILE // CHUNK
        x_refs, prefix_ref, refs = refs[:parts], refs[parts], refs[parts + 1:]
        head = jnp.where(pl.program_id(1) == 0, prefix_ref[...], x_refs[0][0])
        x = jnp.concatenate([head] + [r[0] for r in x_refs[1:]], axis=0)
        refs[-1][0] = x
        refs = refs[:-1]
    else:
        x = refs[0][0]
        refs = refs[1:]
    (g1_ref, w_ref, hg_ref, bd_ref, cos_ref, sin_ref,
     sq_ref, sk_ref, sv_ref, rq_ref, rk_ref, rv_ref, rg_ref, u_ref) = refs
    ms = jnp.mean(x * x, axis=-1, keepdims=True)
    hn = (x * lax.rsqrt(ms + RMS_EPS) * g1_ref[...]).astype(BF16)

    def proj_pair(seg):
        p = _dot(hn, w_ref[:, seg * SB_WIDTH:min((seg + 2) * SB_WIDTH, D_IN)])
        return p[:, :SB_WIDTH], p[:, SB_WIDTH:]

    def head_norm_pair(a, b, gi):
        p = jnp.concatenate([a, b], axis=1)
        sq = (p * p).astype(BF16)
        wide = bd_ref.shape[0]
        m = jnp.concatenate(
            [_dot(sq[:, c:c + wide], bd_ref[...]) for c in range(0, 2 * SB_WIDTH, wide)], axis=1)
        gains = jnp.concatenate([hg_ref[gi:gi + 1, :], hg_ref[gi + 1:gi + 2, :]], axis=1)
        p = p * lax.rsqrt(m + RMS_EPS) * gains
        return p[:, :SB_WIDTH], p[:, SB_WIDTH:]

    lane = lax.broadcasted_iota(jnp.int32, (1, LANES), 1)
    lower = (lane & (HEAD_DIM - 1)) < HEAD_DIM // 2
    cos = cos_ref[...]
    sin = sin_ref[...]

    def rotary(t):
        outs = []
        for c in range(HEAD_PAIRS):
            tc = t[:, c * LANES:(c + 1) * LANES]
            partner = jnp.where(lower, pltpu.roll(tc, LANES - HEAD_DIM // 2, 1),
                                pltpu.roll(tc, HEAD_DIM // 2, 1))
            outs.append(tc * cos + partner * sin)
        return jnp.concatenate(outs, axis=1)

    wide_lane = lax.broadcasted_iota(jnp.int32, (1, SB_WIDTH), 1)
    first_head = (wide_lane & (LANES - 1)) < HEAD_DIM

    def store_per_head(ref, t):
        t = t.astype(BF16)
        zero = jnp.zeros_like(t)
        ref[0, 0] = jnp.where(first_head, t, zero)
        ref[0, 1] = jnp.where(first_head, zero, t)

    sq, sk = head_norm_pair(*proj_pair(0), 0)
    sq_ref[0] = (sq * SB_QUERY_SCALE).astype(BF16)
    store_per_head(sk_ref, sk)
    sv, rq = proj_pair(2)
    store_per_head(sv_ref, sv)
    rk, rv = proj_pair(4)
    rv_ref[0] = rv.astype(BF16)
    rq, rk = head_norm_pair(rq, rk, 2)
    rq_ref[0] = rotary(rq)
    rk_ref[0] = rotary(rk)
    rg_ref[0], u_ref[...] = proj_pair(6)


def _in_proj(h, g1, w_in, head_gains, bd, cos_t, sin_t, tm, prefix=None):
    from_tokens = prefix is not None
    bsz = h.shape[0]
    if from_tokens:
        tm = TOKEN_TILE
        L = h.shape[1] + CHUNK
        parts = tm // CHUNK
        act_specs = [pl.BlockSpec((1, CHUNK, D_MODEL), lambda b, t, j=j: (b, jnp.maximum(parts * t - 1 + j, 0), 0))
                     for j in range(parts)]
        act_specs.append(pl.BlockSpec((CHUNK, D_MODEL), lambda b, t: (0, 0)))
        acts = [h] * parts + [prefix]
    else:
        L = h.shape[1]
        act_specs = [pl.BlockSpec((1, tm, D_MODEL), lambda b, t: (b, t, 0))]
        acts = [h]
    grid = (bsz, L // tm)
    row3 = lambda w: pl.BlockSpec((1, tm, w), lambda b, t: (b, t, 0))
    const = lambda shape: pl.BlockSpec(shape, lambda b, t: (0,) * len(shape), pipeline_mode=pl.Buffered(1))
    out_shapes = [
        jax.ShapeDtypeStruct((bsz, L, SB_WIDTH), BF16),
        jax.ShapeDtypeStruct((bsz, 2, L, SB_WIDTH), BF16),
        jax.ShapeDtypeStruct((bsz, 2, L, SB_WIDTH), BF16),
        jax.ShapeDtypeStruct((bsz, L, RET_WIDTH), F32),
        jax.ShapeDtypeStruct((bsz, L, RET_WIDTH), F32),
        jax.ShapeDtypeStruct((bsz, L, RET_WIDTH), BF16),
        jax.ShapeDtypeStruct((bsz, L, RET_WIDTH), F32),
        jax.ShapeDtypeStruct((L, bsz * S5_WIDTH), F32),
    ]
    per_head = pl.BlockSpec((1, 2, tm, SB_WIDTH), lambda b, t: (b, 0, t, 0))
    out_specs = ([row3(SB_WIDTH), per_head, per_head] + [row3(RET_WIDTH)] * 4
                 + [pl.BlockSpec((tm, S5_WIDTH), lambda b, t: (t, b))])
    if from_tokens:
        out_shapes.append(jax.ShapeDtypeStruct((bsz, L, D_MODEL), F32))
        out_specs.append(row3(D_MODEL))
    return pl.pallas_call(
        functools.partial(_in_kernel, from_tokens=from_tokens),
        grid=grid,
        in_specs=[
            *act_specs,
            const((1, D_MODEL)),
            const((D_MODEL, D_IN)),
            const((4, SB_WIDTH)),
            const(bd.shape),
            pl.BlockSpec((tm, LANES), lambda b, t: (t, 0)),
            pl.BlockSpec((tm, LANES), lambda b, t: (t, 0)),
        ],
        out_specs=out_specs,
        out_shape=out_shapes,
        compiler_params=pltpu.CompilerParams(
            dimension_semantics=("arbitrary", "arbitrary"), vmem_limit_bytes=VMEM_LIMIT_BYTES),
        name="in_proj",
    )(*acts, g1, w_in, head_gains, bd, cos_t, sin_t)


LAST_SLOT = 2


def _sb_kernel(q_ref, k_ref, v_ref, tri_ref, o_ref, acc_ref, c_ref, accp_ref, cp_ref, sp_ref, z_ref, v0_ref,
               *, tq, tk):
    qi = pl.program_id(1)
    n_q_blocks = pl.num_programs(1) - 1
    n_key_blocks = k_ref.shape[2] // tk
    width = 2 * tk
    t_idx = qi * tq + lax.broadcasted_iota(jnp.int32, (tq, 1), 0)
    s_lane = lax.broadcasted_iota(jnp.int32, (1, width), 1) & (tk - 1)

    n_pairs = ((qi + 1) * (tq // tk) + 1) // 2

    def block_rows(j):
        return pl.ds(pl.multiple_of(jnp.minimum(j, n_key_blocks - 1) * tk, tk), tk)

    def lanes(p):
        return slice(p * LANES, (p + 1) * LANES)

    def per_head_rows(ref, j, p):
        return jnp.concatenate([ref[0, 0, block_rows(j), lanes(p)], ref[0, 1, block_rows(j), lanes(p)]], axis=0)

    def score_dots(m, p):
        q = q_ref[0, :, lanes(p)]
        return [_dot_nt(q, per_head_rows(k_ref, j, p)) for j in (2 * m + 1, 2 * m)]

    def score_logs(zs, m, p, slot, causal, padded):
        for half, (z, j) in enumerate(zip(zs, (2 * m + 1, 2 * m))):
            if causal or padded:
                s_idx = j * tk + s_lane
                mask = s_idx < t_idx if causal else s_idx >= PAD_FRONT
                if causal and padded:
                    mask = mask & (s_idx >= PAD_FRONT)
                z = jnp.where(mask, z, MASKED_LOG2_WEIGHT)
            neg_abs = lax.bitcast_convert_type(
                lax.bitcast_convert_type(z, jnp.uint32) | jnp.uint32(0x80000000), F32)
            sp = jnp.maximum(z, 0.0) + jnp.log2(1.0 + jnp.exp2(neg_abs))
            cols = slice(half * width, (half + 1) * width)
            sp_ref[p, slot, :, cols] = sp.astype(BF16)
            z_ref[p, slot, :, cols] = z

    def weights(slot, p, c_in_ref, c_out_ref):
        c = c_in_ref[p]
        ws = []
        for half in range(2):
            cols = slice(half * width, (half + 1) * width)
            later = _dot(sp_ref[p, slot, :, cols], tri_ref[...]) + c
            ws.append(jnp.exp2(z_ref[p, slot, :, cols] - later).astype(BF16))
            c = jnp.concatenate([jnp.broadcast_to(later[:, 0:1], (tq, tk)),
                                 jnp.broadcast_to(later[:, tk:tk + 1], (tq, tk))], axis=1)
        if c_out_ref is not None:
            c_out_ref[p] = c
        return jnp.concatenate(ws, axis=1)

    def weighted_values(w, m, p):
        vv = jnp.concatenate([per_head_rows(v_ref, j, p) for j in (2 * m + 1, 2 * m)], axis=0)
        return _dot(w, vv)

    def weighted_values_pair0(w, p):
        vv = jnp.concatenate([v0_ref[head, rows, lanes(p)]
                              for rows in (slice(tk, 2 * tk), slice(0, tk)) for head in (0, 1)], axis=0)
        return _dot(w, vv)

    def step(m, slot, causal, padded, last=False):
        weight_slot = 1 if last else 1 - slot
        c_out_ref, acc_out_ref = (cp_ref, accp_ref) if last else (c_ref, acc_ref)
        for p in range(HEAD_PAIRS):
            score_logs(score_dots(m, p), m, p, slot, causal, padded)
            w = weights(weight_slot, p, c_ref, c_out_ref)
            acc_out_ref[:, lanes(p)] = acc_ref[:, lanes(p)] + weighted_values(w, m + 1, p)
        if last:
            v0_ref[...] = v_ref[0, :, 0:2 * tk, :]

    def finish_previous(p):
        w = weights(LAST_SLOT, p, cp_ref, None)
        o_ref[0, :, lanes(p)] = (accp_ref[:, lanes(p)] + weighted_values_pair0(w, p)).astype(BF16)

    @pl.when(qi == 0)
    def _():
        acc_ref[...] = jnp.zeros_like(acc_ref)
        c_ref[...] = jnp.zeros_like(c_ref)
        for p in range(HEAD_PAIRS):
            score_logs(score_dots(1, p), 1, p, 1, True, False)
        step(0, LAST_SLOT, True, True, last=True)

    @pl.when(qi == n_q_blocks)
    def _():
        for p in range(HEAD_PAIRS):
            finish_previous(p)

    @pl.when(jnp.logical_and(qi > 0, qi < n_q_blocks))
    def _():
        for parity in (0, 1):
            @pl.when((n_pairs - 1) % 2 == parity)
            def _():
                acc_ref[...] = jnp.zeros_like(acc_ref)
                c_ref[...] = jnp.zeros_like(c_ref)
                for p in range(HEAD_PAIRS):
                    finish_previous(p)
                    score_logs(score_dots(n_pairs - 1, p), n_pairs - 1, p, parity, True, False)
                step(n_pairs - 2, 1 - parity, True, False)

        @pl.when(n_pairs % 2 == 0)
        def _():
            step(n_pairs - 3, 1, False, False)

        n_double = (n_pairs - 3) // 2

        def body(i, carry):
            m = 2 * (n_double - i)
            step(m, 0, False, False)
            step(m - 1, 1, False, False)
            return carry

        lax.fori_loop(0, n_double, body, 0)
        step(0, LAST_SLOT, False, True, last=True)


def _sb_attention(q, k, v, tri, tq, tk):
    bsz, L, _ = q.shape
    assert tq % tk == 0 and tq // tk >= 3, "the pipeline prologue needs two key-block pairs per query block"
    n_q_blocks = L // tq
    slot_shape = (HEAD_PAIRS, 3, tq, 4 * tk)

    def kv_index(b, i):
        return (jnp.where(i == n_q_blocks, jnp.minimum(b + 1, bsz - 1), b), 0, 0, 0)

    return pl.pallas_call(
        functools.partial(_sb_kernel, tq=tq, tk=tk),
        grid=(bsz, n_q_blocks + 1),
        in_specs=[
            pl.BlockSpec((1, tq, SB_WIDTH), lambda b, i: (b, jnp.minimum(i, n_q_blocks - 1), 0)),
            pl.BlockSpec((1, 2, L, SB_WIDTH), kv_index), pl.BlockSpec((1, 2, L, SB_WIDTH), kv_index),
            pl.BlockSpec((2 * tk, 2 * tk), lambda b, i: (0, 0)),
        ],
        out_specs=pl.BlockSpec((1, tq, SB_WIDTH), lambda b, i: (b, jnp.maximum(i - 1, 0), 0)),
        out_shape=jax.ShapeDtypeStruct((bsz, L, SB_WIDTH), BF16),
        scratch_shapes=[pltpu.VMEM((tq, SB_WIDTH), F32), pltpu.VMEM((HEAD_PAIRS, tq, 2 * tk), F32),
                        pltpu.VMEM((tq, SB_WIDTH), F32), pltpu.VMEM((HEAD_PAIRS, tq, 2 * tk), F32),
                        pltpu.VMEM(slot_shape, BF16), pltpu.VMEM(slot_shape, F32),
                        pltpu.VMEM((2, 2 * tk, SB_WIDTH), BF16)],
        compiler_params=pltpu.CompilerParams(
            dimension_semantics=("arbitrary", "arbitrary"), vmem_limit_bytes=VMEM_LIMIT_BYTES),
        name="sb_attention",
    )(q, k, v, tri)


def _ret_kernel(q_ref, k_ref, v_ref, g_ref, dq_ref, dk_ref, din_ref, cd_ref, go_ref, bd_ref,
                o_ref, s_ref, *, chunks):
    @pl.when(pl.program_id(1) == 0)
    def _():
        s_ref[...] = jnp.zeros_like(s_ref)

    first = _first_head_lanes()
    same_head = (lax.broadcasted_iota(jnp.int32, (LANES, 1), 0) < HEAD_DIM) == first
    units = [(c, p) for c in range(chunks) for p in range(HEAD_PAIRS)]

    def block(ref, c, p):
        return ref[0, c * CHUNK:(c + 1) * CHUNK, p * LANES:(p + 1) * LANES]

    q = {u: block(q_ref, *u) for u in units}
    k = {u: block(k_ref, *u) * (HEAD_DIM ** -0.5) for u in units}
    v = {u: block(v_ref, *u) for u in units}
    scores = {(c, p): _dot_nt(q[c, p].astype(BF16), _head_split(k[c, p].astype(BF16), first)) * din_ref[p]
              for c, p in units}
    inner = {u: _dot(scores[u].astype(BF16), _head_split(v[u], first)) for u in units}
    upd = {(c, p): _dot_tn((k[c, p] * dk_ref[p]).astype(BF16), v[c, p]) for c, p in units}
    out = {}
    for p in range(HEAD_PAIRS):
        state = s_ref[p]
        for c in range(chunks):
            cross = _dot((q[c, p] * dq_ref[p]).astype(BF16), state.astype(BF16))
            state = state * cd_ref[p] + jnp.where(same_head, upd[c, p], 0.0)
            out[c, p] = inner[c, p] + cross
        s_ref[p] = state
    for c, p in units:
        o = out[c, p]
        ms = _split_dot(o * o, bd_ref[...])
        on = o * lax.rsqrt(ms + RMS_EPS) * go_ref[:, p * LANES:(p + 1) * LANES]
        g = block(g_ref, c, p)
        o_ref[0, c * CHUNK:(c + 1) * CHUNK, p * LANES:(p + 1) * LANES] = (
            on * (g * jax.nn.sigmoid(g))).astype(BF16)


def _retention(q, k, v, g, dq, dk, din, cd, gout, bd, chunks):
    bsz, L, _ = q.shape
    rows = chunks * CHUNK
    grid = (bsz, L // rows)
    blk = pl.BlockSpec((1, rows, RET_WIDTH), lambda b, n: (b, n, 0))
    const = lambda shape: pl.BlockSpec(shape, lambda b, n: (0,) * len(shape))
    return pl.pallas_call(
        functools.partial(_ret_kernel, chunks=chunks),
        grid=grid,
        in_specs=[
            blk, blk, blk, blk,
            const((HEAD_PAIRS, CHUNK, LANES)), const((HEAD_PAIRS, CHUNK, LANES)),
            const((HEAD_PAIRS, CHUNK, 2 * CHUNK)), const((HEAD_PAIRS, 1, LANES)),
            const((1, RET_WIDTH)), const((LANES, LANES)),
        ],
        out_specs=blk,
        out_shape=jax.ShapeDtypeStruct((bsz, L, RET_WIDTH), BF16),
        scratch_shapes=[pltpu.VMEM((HEAD_PAIRS, LANES, LANES), F32)],
        compiler_params=pltpu.CompilerParams(dimension_semantics=("arbitrary", "arbitrary")),
        name="retention",
    )(q, k, v, g, dq, dk, din, cd, gout, bd)


def _s5_kernel(u_ref, u_old_ref, bt_ref, ar_ref, ai_ref, ct_ref, d_ref, wg_ref, o_ref,
               bu0_ref, bu1_ref, x0_ref, x1_ref, st_ref, ut_ref, uo_ref, ot_ref, *, steps, bsz, slab):
    i = pl.program_id(0)
    bu_refs = (bu0_ref, bu1_ref)
    x_refs = (x0_ref, x1_ref)

    @pl.when(i == 0)
    def _():
        for ref in (st_ref, bu0_ref, bu1_ref, x0_ref, x1_ref):
            ref[...] = jnp.zeros_like(ref)

    def body(slot):
        bu_new_ref, bu_ref = bu_refs[slot], bu_refs[1 - slot]
        x_old_ref, x_ref = x_refs[slot], x_refs[1 - slot]
        planes = S5_WIDTH // LANES
        for b in range(bsz):
            for h in range(planes):
                chans = slice(b * S5_WIDTH + h * LANES, b * S5_WIDTH + (h + 1) * LANES)
                ut_ref[h, pl.ds(b, steps, stride=bsz), :] = u_ref[:, chans]
                uo_ref[h, pl.ds(b, steps, stride=bsz), :] = u_old_ref[:, chans]
        u_old = jnp.concatenate([uo_ref[h] for h in range(planes)], axis=1)
        u_new = jnp.concatenate([ut_ref[h] for h in range(planes)], axis=1).astype(BF16)
        y = d_ref[...] * u_old
        slabs = S5_NSTATE // slab
        runs_per_slab = S5_INTERLEAVE // slabs
        run = steps // runs_per_slab
        kw = 2 * S5_NSTATE // S5_INTERLEAVE
        for piece in range(S5_INTERLEAVE):
            ls, part = divmod(piece, runs_per_slab)
            re = slice(ls * slab, (ls + 1) * slab)
            im = slice(S5_NSTATE + ls * slab, S5_NSTATE + (ls + 1) * slab)
            ar = jnp.broadcast_to(ar_ref[:, re], (bsz, slab))
            ai = jnp.broadcast_to(ai_ref[:, re], (bsz, slab))
            xr = st_ref[:, re]
            xi = st_ref[:, im]
            for t in range(part * run, (part + 1) * run):
                rows = slice(t * bsz, (t + 1) * bsz)
                xr, xi = (ar * xr - ai * xi + bu_ref[rows, re],
                          ar * xi + ai * xr + bu_ref[rows, im])
                x_ref[rows, re] = xr.astype(BF16)
                x_ref[rows, im] = xi.astype(BF16)
            st_ref[:, re] = xr
            st_ref[:, im] = xi

            cols = slice(piece * kw, (piece + 1) * kw)
            y = y + _dot(x_old_ref[:, cols], ct_ref[cols, :])
            bu_new_ref[:, cols] = _dot(u_new, bt_ref[:, cols])

        y = jax.nn.gelu(y)
        out = y * jax.nn.sigmoid(_dot(y.astype(BF16), wg_ref[...]))
        for h in range(planes):
            ot_ref[h] = out[:, h * LANES:(h + 1) * LANES]
        for b in range(bsz):
            for h in range(planes):
                chans = slice(b * S5_WIDTH + h * LANES, b * S5_WIDTH + (h + 1) * LANES)
                o_ref[:, chans] = ot_ref[h, pl.ds(b, steps, stride=bsz), :].astype(BF16)

    for slot in (0, 1):
        @pl.when(i % 2 == slot)
        def _():
            body(slot)


def _s5(u, bt, ar, ai, ct, d, wg, bsz, steps):
    L = u.shape[0]
    blk_rows = steps * bsz
    n_blocks = L // steps
    io_block = (steps, bsz * S5_WIDTH)
    const = lambda shape: pl.BlockSpec(shape, lambda t: (0, 0))
    return pl.pallas_call(
        functools.partial(_s5_kernel, steps=steps, bsz=bsz, slab=512),
        grid=(n_blocks + 2,),
        in_specs=[
            pl.BlockSpec(io_block, lambda t: (jnp.minimum(t, n_blocks - 1), 0)),
            pl.BlockSpec(io_block, lambda t: (jnp.maximum(t - 2, 0), 0)),
            const((S5_WIDTH, 2 * S5_NSTATE)),
            const((1, S5_NSTATE)), const((1, S5_NSTATE)),
            const((2 * S5_NSTATE, S5_WIDTH)),
            const((1, S5_WIDTH)),
            const((S5_WIDTH, S5_WIDTH)),
        ],
        out_specs=pl.BlockSpec(io_block, lambda t: (jnp.maximum(t - 2, 0), 0)),
        out_shape=jax.ShapeDtypeStruct((L, bsz * S5_WIDTH), BF16),
        scratch_shapes=[
            pltpu.VMEM((blk_rows, 2 * S5_NSTATE), F32), pltpu.VMEM((blk_rows, 2 * S5_NSTATE), F32),
            pltpu.VMEM((blk_rows, 2 * S5_NSTATE), BF16), pltpu.VMEM((blk_rows, 2 * S5_NSTATE), BF16),
            pltpu.VMEM((bsz, 2 * S5_NSTATE), F32),
            pltpu.VMEM((S5_WIDTH // LANES, blk_rows, LANES), F32),
            pltpu.VMEM((S5_WIDTH // LANES, blk_rows, LANES), F32),
            pltpu.VMEM((S5_WIDTH // LANES, blk_rows, LANES), F32),
        ],
        compiler_params=pltpu.CompilerParams(
            dimension_semantics=("arbitrary",), vmem_limit_bytes=VMEM_LIMIT_BYTES),
        name="s5_mixer",
    )(u, u, bt, ar, ai, ct, d, wg)


FF_TILE_STARTS = (0, 768, 1536, 2304, D_FF)
FINAL_ROW_TILE = 512


def _out_ffn_kernel(*refs, rows, parts, drop_prefix):
    n_act = 4 * parts
    h_refs, sb_refs, ro_refs, so_refs = (refs[i * parts:(i + 1) * parts] for i in range(4))
    wo_ref, g2_ref, wg_ref, wu_ref, wd_ref, o_ref = refs[n_act:]
    stack = lambda chunks: chunks[0] if parts == 1 else jnp.concatenate(chunks, axis=0)
    mix = jnp.concatenate([stack([r[0] for r in sb_refs]), stack([r[0] for r in ro_refs]),
                           stack([r[...] for r in so_refs])], axis=1)
    h1 = stack([r[0] for r in h_refs]) + _dot(mix, wo_ref[...])
    ms = jnp.mean(h1 * h1, axis=-1, keepdims=True)
    hn = (h1 * lax.rsqrt(ms + RMS_EPS) * g2_ref[...]).astype(BF16)
    ffn = jnp.zeros_like(h1)
    for lo, hi in zip(FF_TILE_STARTS[:-1], FF_TILE_STARTS[1:]):
        gate = _dot(hn, wg_ref[:, lo:hi])
        up = _dot(hn, wu_ref[:, lo:hi])
        act = (gate * jax.nn.sigmoid(gate)) * up
        ffn = ffn + _dot(act.astype(BF16), wd_ref[lo:hi, :])
    h2 = h1 + ffn
    if drop_prefix:
        o_ref[0] = h2
    else:
        pos = pl.program_id(1) * rows + lax.broadcasted_iota(jnp.int32, (rows, 1), 0)
        o_ref[0] = jnp.where(pos >= PAD_FRONT, h2, 0.0)


def _out_ffn(h, sb_o, ro, so_tm, w_out, g2, w_gate, w_up, w_down, tm, drop_prefix=False):
    bsz, L, _ = h.shape
    if drop_prefix:
        rows, parts, first_chunk = FINAL_ROW_TILE, FINAL_ROW_TILE // CHUNK, 1
        out_rows = L - CHUNK
    else:
        rows, parts, first_chunk = tm, 1, 0
        out_rows = L
    chunk = rows // parts
    act3 = lambda w: [pl.BlockSpec((1, chunk, w), lambda b, t, j=j: (b, first_chunk + parts * t + j, 0))
                      for j in range(parts)]
    so_specs = [pl.BlockSpec((chunk, S5_WIDTH), lambda b, t, j=j: (first_chunk + parts * t + j, b))
                for j in range(parts)]
    const = lambda shape: pl.BlockSpec(shape, lambda b, t: (0, 0), pipeline_mode=pl.Buffered(1))
    return pl.pallas_call(
        functools.partial(_out_ffn_kernel, rows=rows, parts=parts, drop_prefix=drop_prefix),
        grid=(bsz, out_rows // rows),
        in_specs=[
            *act3(D_MODEL), *act3(SB_WIDTH), *act3(RET_WIDTH), *so_specs,
            const((D_MODEL, D_MODEL)), const((1, D_MODEL)),
            const((D_MODEL, D_FF)), const((D_MODEL, D_FF)), const((D_FF, D_MODEL)),
        ],
        out_specs=pl.BlockSpec((1, rows, D_MODEL), lambda b, t: (b, t, 0)),
        out_shape=jax.ShapeDtypeStruct((bsz, out_rows, D_MODEL), F32),
        compiler_params=pltpu.CompilerParams(
            dimension_semantics=("arbitrary", "arbitrary"), vmem_limit_bytes=VMEM_LIMIT_BYTES),
        name="out_ffn",
    )(*([h] * parts), *([sb_o] * parts), *([ro] * parts), *([so_tm] * parts),
      w_out, g2, w_gate, w_up, w_down)


def _block_diag_mean(width):
    idx = jnp.arange(width) // HEAD_DIM
    return jnp.where(idx[:, None] == idx[None, :], 1.0 / HEAD_DIM, 0.0).astype(BF16)


def _suffix_count_matrix(tk):
    idx = jnp.arange(2 * tk)
    same = (idx[:, None] // tk) == (idx[None, :] // tk)
    return jnp.where(same & (idx[:, None] >= idx[None, :]), 1.0, 0.0).astype(BF16)


def _rope_tables(L):
    half = HEAD_DIM // 2
    pos = (jnp.arange(L) - PAD_FRONT).astype(F32)
    inv = ROPE_BASE ** (-jnp.arange(half, dtype=F32) / half)
    ang = pos[:, None] * inv[None, :]
    cos = jnp.cos(ang)
    sin = jnp.sin(ang)
    cos_t = jnp.tile(jnp.concatenate([cos, cos], axis=1), (1, LANES // HEAD_DIM))
    sin_t = jnp.tile(jnp.concatenate([-sin, sin], axis=1), (1, LANES // HEAD_DIM))
    return cos_t, sin_t


def _retention_tables():
    log_gamma = jnp.log1p(-jnp.exp2(-5.0 - jnp.arange(RET_HEADS, dtype=F32)))
    i = jnp.arange(CHUNK, dtype=F32)
    diff = i[:, None] - i[None, :]
    decay_in = jnp.where(diff >= 0, jnp.exp(log_gamma[:, None, None] * jnp.maximum(diff, 0.0)), 0.0)
    q_decay = jnp.exp(log_gamma[:, None] * (i + 1.0))
    k_decay = jnp.exp(log_gamma[:, None] * (CHUNK - 1.0 - i))
    chunk_decay = jnp.exp(log_gamma * CHUNK)
    per_lane = lambda t: jnp.repeat(t.T.reshape(CHUNK, HEAD_PAIRS, 2), HEAD_DIM, axis=2).transpose(1, 0, 2)
    dq = per_lane(q_decay)
    dk = per_lane(k_decay)
    din = decay_in.reshape(HEAD_PAIRS, 2, CHUNK, CHUNK).transpose(0, 2, 1, 3).reshape(HEAD_PAIRS, CHUNK, 2 * CHUNK)
    cd = jnp.repeat(chunk_decay.reshape(HEAD_PAIRS, 1, 2), HEAD_DIM, axis=2)
    return dq, dk, din, cd


def _s5_params(lam_re, lam_im, log_dt, b_re, b_im, c_re, c_im):
    dt = jnp.exp(log_dt)[:, None]
    mag = jnp.exp(lam_re * dt)
    ar = mag * jnp.cos(lam_im * dt)
    ai = mag * jnp.sin(lam_im * dt)
    den = lam_re * lam_re + lam_im * lam_im
    fr = ((ar - 1.0) * lam_re + ai * lam_im) / den
    fi = (ai * lam_re - (ar - 1.0) * lam_im) / den
    bbr = fr[..., None] * b_re - fi[..., None] * b_im
    bbi = fr[..., None] * b_im + fi[..., None] * b_re
    eye = jnp.eye(S5_GROUPS, dtype=F32)
    expand_b = lambda m: jnp.einsum('gpc,gh->gchp', m, eye).reshape(S5_WIDTH, S5_NSTATE)
    bt = jnp.concatenate([expand_b(bbr), expand_b(bbi)], axis=1).astype(BF16)
    expand_c = lambda m: jnp.einsum('gcp,gh->hpgc', m, eye).reshape(S5_NSTATE, S5_WIDTH)
    ct = jnp.concatenate([expand_c(c_re), -expand_c(c_im)], axis=0).astype(BF16)
    return bt, ar.reshape(1, S5_NSTATE), ai.reshape(1, S5_NSTATE), ct


def _largest_tile(L, candidates):
    for c in candidates:
        if L % c == 0:
            return c
    raise ValueError(f"no supported tile divides sequence length {L}")


def kernel(x, meta_tokens, norm1_g, w_in, sb_q_g, sb_k_g, ret_q_g, ret_k_g, ret_out_g, s5_lam_re, s5_lam_im,
           s5_log_dt, s5_b_re, s5_b_im, s5_c_re, s5_c_im, s5_d, s5_w_glu, w_out, norm2_g, w_gate, w_up, w_down):
    bsz, seq, _ = x.shape
    depth = w_in.shape[0]
    L = PAD_FRONT + N_META + seq
    assert L % CHUNK == 0
    tm = _largest_tile(L, (528, 384, 128))
    tq = _largest_tile(L, (384,))
    s5_steps = _largest_tile(L, (32, 16, 8))

    prefix = jnp.concatenate([jnp.zeros((PAD_FRONT, D_MODEL), x.dtype), meta_tokens.astype(x.dtype)], axis=0)
    h = x

    bd128 = _block_diag_mean(LANES)
    bd256 = _block_diag_mean(2 * LANES)
    tri = _suffix_count_matrix(CHUNK)
    cos_t, sin_t = _rope_tables(L)
    dq, dk, din, cd = _retention_tables()
    tile6 = lambda g: jnp.tile(g, SB_HEADS)

    for l in range(depth):
        head_gains = jnp.stack([tile6(sb_q_g[l]), tile6(sb_k_g[l]), tile6(ret_q_g[l]), tile6(ret_k_g[l])])
        outs = _in_proj(h, norm1_g[l][None], w_in[l].astype(BF16), head_gains, bd256, cos_t, sin_t, tm,
                        prefix=prefix if l == 0 else None)
        if l == 0:
            h = outs[-1]
        sq, sk, sv, rq, rk, rv, rg, u = outs[:8]
        sb_o = _sb_attention(sq, sk, sv, tri, tq, CHUNK)
        ro = _retention(rq, rk, rv, rg, dq, dk, din, cd, ret_out_g[l][None], bd128, tq // CHUNK)
        bt, ar, ai, ct = _s5_params(s5_lam_re[l], s5_lam_im[l], s5_log_dt[l], s5_b_re[l], s5_b_im[l],
                                    s5_c_re[l], s5_c_im[l])
        so = _s5(u, bt, ar, ai, ct, s5_d[l][None], s5_w_glu[l].astype(BF16), bsz, s5_steps)
        h = _out_ffn(h, sb_o, ro, so, w_out[l].astype(BF16), norm2_g[l][None],
                     w_gate[l].astype(BF16), w_up[l].astype(BF16), w_down[l].astype(BF16), tm,
                     drop_prefix=(l == depth - 1 and seq % FINAL_ROW_TILE == 0))
    return h if h.shape[1] == seq else h[:, PAD_FRONT + N_META:]
```

```python
import functools
import math

import jax
import jax.numpy as jnp
from jax import lax
from jax.experimental import pallas as pl
from jax.experimental.pallas import tpu as pltpu

D_MODEL = 1024
HEAD_DIM = 64
SB_HEADS = 6
RET_HEADS = 6
SB_WIDTH = SB_HEADS * HEAD_DIM
RET_WIDTH = RET_HEADS * HEAD_DIM
S5_WIDTH = D_MODEL - SB_WIDTH - RET_WIDTH
S5_GROUP_CH = 16
S5_GROUPS = S5_WIDTH // S5_GROUP_CH
S5_STATE = 64
S5_NSTATE = S5_GROUPS * S5_STATE
D_IN = 3 * SB_WIDTH + 4 * RET_WIDTH + S5_WIDTH
D_FF = 2816
CHUNK = 128
N_META = 16
PAD_FRONT = CHUNK - N_META
ROPE_BASE = 10000.0
RMS_EPS = 1e-6
SB_QUERY_SCALE = HEAD_DIM ** -0.5 * math.log2(math.e)
S5_INTERLEAVE = 4
MASKED_LOG2_WEIGHT = -1e30

LANES = 128
HEAD_PAIRS = SB_HEADS // 2
VMEM_LIMIT_BYTES = 56 * 1024 * 1024

F32 = jnp.float32
BF16 = jnp.bfloat16


def _dot(a, b):
    return jnp.dot(a, b, preferred_element_type=F32)


def _dot_nt(a, b):
    return lax.dot_general(a, b, (((1,), (1,)), ((), ())), preferred_element_type=F32)


def _dot_tn(a, b):
    return lax.dot_general(a, b, (((0,), (0,)), ((), ())), preferred_element_type=F32)


def _split_dot(x, m):
    hi = x.astype(BF16)
    lo = (x - hi.astype(F32)).astype(BF16)
    return _dot(hi, m) + _dot(lo, m)


def _head_split(t, first_head):
    zero = jnp.zeros_like(t)
    return jnp.concatenate([jnp.where(first_head, t, zero), jnp.where(first_head, zero, t)], axis=0)


def _first_head_lanes():
    return lax.broadcasted_iota(jnp.int32, (1, LANES), 1) < HEAD_DIM


TOKEN_TILE = 3 * CHUNK


def _in_kernel(*refs, from_tokens):
    if from_tokens:
        parts = TOKEN_TILE // CHUNK
        x_refs, prefix_ref, refs = refs[:parts], refs[parts], refs[parts + 1:]
        head = jnp.where(pl.program_id(1) == 0, prefix_ref[...], x_refs[0][0])
        x = jnp.concatenate([head] + [r[0] for r in x_refs[1:]], axis=0)
        refs[-1][0] = x
        refs = refs[:-1]
    else:
        x = refs[0][0]
        refs = refs[1:]
    (g1_ref, w_ref, hg_ref, bd_ref, cos_ref, sin_ref,
     sq_ref, sk_ref, sv_ref, rq_ref, rk_ref, rv_ref, rg_ref, u_ref) = refs
    ms = jnp.mean(x * x, axis=-1, keepdims=True)
    hn = (x * lax.rsqrt(ms + RMS_EPS) * g1_ref[...]).astype(BF16)

    def proj_pair(seg):
        p = _dot(hn, w_ref[:, seg * SB_WIDTH:min((seg + 2) * SB_WIDTH, D_IN)])
        return p[:, :SB_WIDTH], p[:, SB_WIDTH:]

    def head_norm_pair(a, b, gi):
        p = jnp.concatenate([a, b], axis=1)
        sq = (p * p).astype(BF16)
        wide = bd_ref.shape[0]
        m = jnp.concatenate(
            [_dot(sq[:, c:c + wide], bd_ref[...]) for c in range(0, 2 * SB_WIDTH, wide)], axis=1)
        gains = jnp.concatenate([hg_ref[gi:gi + 1, :], hg_ref[gi + 1:gi + 2, :]], axis=1)
        p = p * lax.rsqrt(m + RMS_EPS) * gains
        return p[:, :SB_WIDTH], p[:, SB_WIDTH:]

    lane = lax.broadcasted_iota(jnp.int32, (1, LANES), 1)
    lower = (lane & (HEAD_DIM - 1)) < HEAD_DIM // 2
    cos = cos_ref[...]
    sin = sin_ref[...]

    def rotary(t):
        outs = []
        for c in range(HEAD_PAIRS):
            tc = t[:, c * LANES:(c + 1) * LANES]
            partner = jnp.where(lower, pltpu.roll(tc, LANES - HEAD_DIM // 2, 1),
                                pltpu.roll(tc, HEAD_DIM // 2, 1))
            outs.append(tc * cos + partner * sin)
        return jnp.concatenate(outs, axis=1)

    wide_lane = lax.broadcasted_iota(jnp.int32, (1, SB_WIDTH), 1)
    first_head = (wide_lane & (LANES - 1)) < HEAD_DIM

    def store_per_head(ref, t):
        t = t.astype(BF16)
        zero = jnp.zeros_like(t)
        ref[0, 0] = jnp.where(first_head, t, zero)
        ref[0, 1] = jnp.where(first_head, zero, t)

    sq, sk = head_norm_pair(*proj_pair(0), 0)
    sq_ref[0] = (sq * SB_QUERY_SCALE).astype(BF16)
    store_per_head(sk_ref, sk)
    sv, rq = proj_pair(2)
    store_per_head(sv_ref, sv)
    rk, rv = proj_pair(4)
    rv_ref[0] = rv.astype(BF16)
    rq, rk = head_norm_pair(rq, rk, 2)
    rq_ref[0] = rotary(rq)
    rk_ref[0] = rotary(rk)
    rg_ref[0], u_ref[...] = proj_pair(6)


def _in_proj(h, g1, w_in, head_gains, bd, cos_t, sin_t, tm, prefix=None):
    from_tokens = prefix is not None
    bsz = h.shape[0]
    if from_tokens:
        tm = TOKEN_TILE
        L = h.shape[1] + CHUNK
        parts = tm // CHUNK
        act_specs = [pl.BlockSpec((1, CHUNK, D_MODEL), lambda b, t, j=j: (b, jnp.maximum(parts * t - 1 + j, 0), 0))
                     for j in range(parts)]
        act_specs.append(pl.BlockSpec((CHUNK, D_MODEL), lambda b, t: (0, 0)))
        acts = [h] * parts + [prefix]
    else:
        L = h.shape[1]
        act_specs = [pl.BlockSpec((1, tm, D_MODEL), lambda b, t: (b, t, 0))]
        acts = [h]
    grid = (bsz, L // tm)
    row3 = lambda w: pl.BlockSpec((1, tm, w), lambda b, t: (b, t, 0))
    const = lambda shape: pl.BlockSpec(shape, lambda b, t: (0,) * len(shape), pipeline_mode=pl.Buffered(1))
    out_shapes = [
        jax.ShapeDtypeStruct((bsz, L, SB_WIDTH), BF16),
        jax.ShapeDtypeStruct((bsz, 2, L, SB_WIDTH), BF16),
        jax.ShapeDtypeStruct((bsz, 2, L, SB_WIDTH), BF16),
        jax.ShapeDtypeStruct((bsz, L, RET_WIDTH), F32),
        jax.ShapeDtypeStruct((bsz, L, RET_WIDTH), F32),
        jax.ShapeDtypeStruct((bsz, L, RET_WIDTH), BF16),
        jax.ShapeDtypeStruct((bsz, L, RET_WIDTH), F32),
        jax.ShapeDtypeStruct((L, bsz * S5_WIDTH), F32),
    ]
    per_head = pl.BlockSpec((1, 2, tm, SB_WIDTH), lambda b, t: (b, 0, t, 0))
    out_specs = ([row3(SB_WIDTH), per_head, per_head] + [row3(RET_WIDTH)] * 4
                 + [pl.BlockSpec((tm, S5_WIDTH), lambda b, t: (t, b))])
    if from_tokens:
        out_shapes.append(jax.ShapeDtypeStruct((bsz, L, D_MODEL), F32))
        out_specs.append(row3(D_MODEL))
    return pl.pallas_call(
        functools.partial(_in_kernel, from_tokens=from_tokens),
        grid=grid,
        in_specs=[
            *act_specs,
            const((1, D_MODEL)),
            const((D_MODEL, D_IN)),
            const((4, SB_WIDTH)),
            const(bd.shape),
            pl.BlockSpec((tm, LANES), lambda b, t: (t, 0)),
            pl.BlockSpec((tm, LANES), lambda b, t: (t, 0)),
        ],
        out_specs=out_specs,
        out_shape=out_shapes,
        compiler_params=pltpu.CompilerParams(
            dimension_semantics=("arbitrary", "arbitrary"), vmem_limit_bytes=VMEM_LIMIT_BYTES),
        name="in_proj",
    )(*acts, g1, w_in, head_gains, bd, cos_t, sin_t)


LAST_SLOT = 2


def _sb_kernel(q_ref, k_ref, v_ref, tri_ref, o_ref, acc_ref, c_ref, accp_ref, cp_ref, sp_ref, z_ref, v0_ref,
               *, tq, tk):
    qi = pl.program_id(1)
    n_q_blocks = pl.num_programs(1) - 1
    n_key_blocks = k_ref.shape[2] // tk
    width = 2 * tk
    t_idx = qi * tq + lax.broadcasted_iota(jnp.int32, (tq, 1), 0)
    s_lane = lax.broadcasted_iota(jnp.int32, (1, width), 1) & (tk - 1)

    n_pairs = ((qi + 1) * (tq // tk) + 1) // 2

    def block_rows(j):
        return pl.ds(pl.multiple_of(jnp.minimum(j, n_key_blocks - 1) * tk, tk), tk)

    def lanes(p):
        return slice(p * LANES, (p + 1) * LANES)

    def per_head_rows(ref, j, p):
        return jnp.concatenate([ref[0, 0, block_rows(j), lanes(p)], ref[0, 1, block_rows(j), lanes(p)]], axis=0)

    def score_dots(m, p):
        q = q_ref[0, :, lanes(p)]
        return [_dot_nt(q, per_head_rows(k_ref, j, p)) for j in (2 * m + 1, 2 * m)]

    def score_logs(zs, m, p, slot, causal, padded):
        for half, (z, j) in enumerate(zip(zs, (2 * m + 1, 2 * m))):
            if causal or padded:
                s_idx = j * tk + s_lane
                mask = s_idx < t_idx if causal else s_idx >= PAD_FRONT
                if causal and padded:
                    mask = mask & (s_idx >= PAD_FRONT)
                z = jnp.where(mask, z, MASKED_LOG2_WEIGHT)
            neg_abs = lax.bitcast_convert_type(
                lax.bitcast_convert_type(z, jnp.uint32) | jnp.uint32(0x80000000), F32)
            sp = jnp.maximum(z, 0.0) + jnp.log2(1.0 + jnp.exp2(neg_abs))
            cols = slice(half * width, (half + 1) * width)
            sp_ref[p, slot, :, cols] = sp.astype(BF16)
            z_ref[p, slot, :, cols] = z

    def weights(slot, p, c_in_ref, c_out_ref):
        c = c_in_ref[p]
        ws = []
        for half in range(2):
            cols = slice(half * width, (half + 1) * width)
            later = _dot(sp_ref[p, slot, :, cols], tri_ref[...]) + c
            ws.append(jnp.exp2(z_ref[p, slot, :, cols] - later).astype(BF16))
            c = jnp.concatenate([jnp.broadcast_to(later[:, 0:1], (tq, tk)),
                                 jnp.broadcast_to(later[:, tk:tk + 1], (tq, tk))], axis=1)
        if c_out_ref is not None:
            c_out_ref[p] = c
        return jnp.concatenate(ws, axis=1)

    def weighted_values(w, m, p):
        vv = jnp.concatenate([per_head_rows(v_ref, j, p) for j in (2 * m + 1, 2 * m)], axis=0)
        return _dot(w, vv)

    def weighted_values_pair0(w, p):
        vv = jnp.concatenate([v0_ref[head, rows, lanes(p)]
                              for rows in (slice(tk, 2 * tk), slice(0, tk)) for head in (0, 1)], axis=0)
        return _dot(w, vv)

    def step(m, slot, causal, padded, last=False):
        weight_slot = 1 if last else 1 - slot
        c_out_ref, acc_out_ref = (cp_ref, accp_ref) if last else (c_ref, acc_ref)
        for p in range(HEAD_PAIRS):
            score_logs(score_dots(m, p), m, p, slot, causal, padded)
            w = weights(weight_slot, p, c_ref, c_out_ref)
            acc_out_ref[:, lanes(p)] = acc_ref[:, lanes(p)] + weighted_values(w, m + 1, p)
        if last:
            v0_ref[...] = v_ref[0, :, 0:2 * tk, :]

    def finish_previous(p):
        w = weights(LAST_SLOT, p, cp_ref, None)
        o_ref[0, :, lanes(p)] = (accp_ref[:, lanes(p)] + weighted_values_pair0(w, p)).astype(BF16)

    @pl.when(qi == 0)
    def _():
        acc_ref[...] = jnp.zeros_like(acc_ref)
        c_ref[...] = jnp.zeros_like(c_ref)
        for p in range(HEAD_PAIRS):
            score_logs(score_dots(1, p), 1, p, 1, True, False)
        step(0, LAST_SLOT, True, True, last=True)

    @pl.when(qi == n_q_blocks)
    def _():
        for p in range(HEAD_PAIRS):
            finish_previous(p)

    @pl.when(jnp.logical_and(qi > 0, qi < n_q_blocks))
    def _():
        for parity in (0, 1):
            @pl.when((n_pairs - 1) % 2 == parity)
            def _():
                acc_ref[...] = jnp.zeros_like(acc_ref)
                c_ref[...] = jnp.zeros_like(c_ref)
                for p in range(HEAD_PAIRS):
                    finish_previous(p)
                    score_logs(score_dots(n_pairs - 1, p), n_pairs - 1, p, parity, True, False)
                step(n_pairs - 2, 1 - parity, True, False)

        @pl.when(n_pairs % 2 == 0)
        def _():
            step(n_pairs - 3, 1, False, False)

        n_double = (n_pairs - 3) // 2

        def body(i, carry):
            m = 2 * (n_double - i)
            step(m, 0, False, False)
            step(m - 1, 1, False, False)
            return carry

        lax.fori_loop(0, n_double, body, 0)
        step(0, LAST_SLOT, False, True, last=True)


def _sb_attention(q, k, v, tri, tq, tk):
    bsz, L, _ = q.shape
    assert tq % tk == 0 and tq // tk >= 3, "the pipeline prologue needs two key-block pairs per query block"
    n_q_blocks = L // tq
    slot_shape = (HEAD_PAIRS, 3, tq, 4 * tk)

    def kv_index(b, i):
        return (jnp.where(i == n_q_blocks, jnp.minimum(b + 1, bsz - 1), b), 0, 0, 0)

    return pl.pallas_call(
        functools.partial(_sb_kernel, tq=tq, tk=tk),
        grid=(bsz, n_q_blocks + 1),
        in_specs=[
            pl.BlockSpec((1, tq, SB_WIDTH), lambda b, i: (b, jnp.minimum(i, n_q_blocks - 1), 0)),
            pl.BlockSpec((1, 2, L, SB_WIDTH), kv_index), pl.BlockSpec((1, 2, L, SB_WIDTH), kv_index),
            pl.BlockSpec((2 * tk, 2 * tk), lambda b, i: (0, 0)),
        ],
        out_specs=pl.BlockSpec((1, tq, SB_WIDTH), lambda b, i: (b, jnp.maximum(i - 1, 0), 0)),
        out_shape=jax.ShapeDtypeStruct((bsz, L, SB_WIDTH), BF16),
        scratch_shapes=[pltpu.VMEM((tq, SB_WIDTH), F32), pltpu.VMEM((HEAD_PAIRS, tq, 2 * tk), F32),
                        pltpu.VMEM((tq, SB_WIDTH), F32), pltpu.VMEM((HEAD_PAIRS, tq, 2 * tk), F32),
                        pltpu.VMEM(slot_shape, BF16), pltpu.VMEM(slot_shape, F32),
                        pltpu.VMEM((2, 2 * tk, SB_WIDTH), BF16)],
        compiler_params=pltpu.CompilerParams(
            dimension_semantics=("arbitrary", "arbitrary"), vmem_limit_bytes=VMEM_LIMIT_BYTES),
        name="sb_attention",
    )(q, k, v, tri)


def _ret_kernel(q_ref, k_ref, v_ref, g_ref, dq_ref, dk_ref, din_ref, cd_ref, go_ref, bd_ref,
                o_ref, s_ref, *, chunks):
    @pl.when(pl.program_id(1) == 0)
    def _():
        s_ref[...] = jnp.zeros_like(s_ref)

    first = _first_head_lanes()
    same_head = (lax.broadcasted_iota(jnp.int32, (LANES, 1), 0) < HEAD_DIM) == first
    units = [(c, p) for c in range(chunks) for p in range(HEAD_PAIRS)]

    def block(ref, c, p):
        return ref[0, c * CHUNK:(c + 1) * CHUNK, p * LANES:(p + 1) * LANES]

    q = {u: block(q_ref, *u) for u in units}
    k = {u: block(k_ref, *u) * (HEAD_DIM ** -0.5) for u in units}
    v = {u: block(v_ref, *u) for u in units}
    scores = {(c, p): _dot_nt(q[c, p].astype(BF16), _head_split(k[c, p].astype(BF16), first)) * din_ref[p]
              for c, p in units}
    inner = {u: _dot(scores[u].astype(BF16), _head_split(v[u], first)) for u in units}
    upd = {(c, p): _dot_tn((k[c, p] * dk_ref[p]).astype(BF16), v[c, p]) for c, p in units}
    out = {}
    for p in range(HEAD_PAIRS):
        state = s_ref[p]
        for c in range(chunks):
            cross = _dot((q[c, p] * dq_ref[p]).astype(BF16), state.astype(BF16))
            state = state * cd_ref[p] + jnp.where(same_head, upd[c, p], 0.0)
            out[c, p] = inner[c, p] + cross
        s_ref[p] = state
    for c, p in units:
        o = out[c, p]
        ms = _split_dot(o * o, bd_ref[...])
        on = o * lax.rsqrt(ms + RMS_EPS) * go_ref[:, p * LANES:(p + 1) * LANES]
        g = block(g_ref, c, p)
        o_ref[0, c * CHUNK:(c + 1) * CHUNK, p * LANES:(p + 1) * LANES] = (
            on * (g * jax.nn.sigmoid(g))).astype(BF16)


def _retention(q, k, v, g, dq, dk, din, cd, gout, bd, chunks):
    bsz, L, _ = q.shape
    rows = chunks * CHUNK
    grid = (bsz, L // rows)
    blk = pl.BlockSpec((1, rows, RET_WIDTH), lambda b, n: (b, n, 0))
    const = lambda shape: pl.BlockSpec(shape, lambda b, n: (0,) * len(shape))
    return pl.pallas_call(
        functools.partial(_ret_kernel, chunks=chunks),
        grid=grid,
        in_specs=[
            blk, blk, blk, blk,
            const((HEAD_PAIRS, CHUNK, LANES)), const((HEAD_PAIRS, CHUNK, LANES)),
            const((HEAD_PAIRS, CHUNK, 2 * CHUNK)), const((HEAD_PAIRS, 1, LANES)),
            const((1, RET_WIDTH)), const((LANES, LANES)),
        ],
        out_specs=blk,
        out_shape=jax.ShapeDtypeStruct((bsz, L, RET_WIDTH), BF16),
        scratch_shapes=[pltpu.VMEM((HEAD_PAIRS, LANES, LANES), F32)],
        compiler_params=pltpu.CompilerParams(
            dimension_semantics=("arbitrary", "arbitrary"), vmem_limit_bytes=VMEM_LIMIT_BYTES),
        name="retention",
    )(q, k, v, g, dq, dk, din, cd, gout, bd)


def _s5_kernel(u_ref, u_old_ref, bt_ref, ar_ref, ai_ref, ct_ref, d_ref, wg_ref, o_ref,
               bu0_ref, bu1_ref, x0_ref, x1_ref, st_ref, ut_ref, uo_ref, ot_ref, *, steps, bsz, slab):
    i = pl.program_id(0)
    bu_refs = (bu0_ref, bu1_ref)
    x_refs = (x0_ref, x1_ref)

    @pl.when(i == 0)
    def _():
        for ref in (st_ref, bu0_ref, bu1_ref, x0_ref, x1_ref):
            ref[...] = jnp.zeros_like(ref)

    def body(slot):
        bu_new_ref, bu_ref = bu_refs[slot], bu_refs[1 - slot]
        x_old_ref, x_ref = x_refs[slot], x_refs[1 - slot]
        planes = S5_WIDTH // LANES
        for b in range(bsz):
            for h in range(planes):
                chans = slice(b * S5_WIDTH + h * LANES, b * S5_WIDTH + (h + 1) * LANES)
                ut_ref[h, pl.ds(b, steps, stride=bsz), :] = u_ref[:, chans]
                uo_ref[h, pl.ds(b, steps, stride=bsz), :] = u_old_ref[:, chans]
        u_old = jnp.concatenate([uo_ref[h] for h in range(planes)], axis=1)
        u_new = jnp.concatenate([ut_ref[h] for h in range(planes)], axis=1).astype(BF16)
        y = d_ref[...] * u_old
        slabs = S5_NSTATE // slab
        runs_per_slab = S5_INTERLEAVE // slabs
        run = steps // runs_per_slab
        kw = 2 * S5_NSTATE // S5_INTERLEAVE
        for piece in range(S5_INTERLEAVE):
            ls, part = divmod(piece, runs_per_slab)
            re = slice(ls * slab, (ls + 1) * slab)
            im = slice(S5_NSTATE + ls * slab, S5_NSTATE + (ls + 1) * slab)
            ar = jnp.broadcast_to(ar_ref[:, re], (bsz, slab))
            ai = jnp.broadcast_to(ai_ref[:, re], (bsz, slab))
            xr = st_ref[:, re]
            xi = st_ref[:, im]
            for t in range(part * run, (part + 1) * run):
                rows = slice(t * bsz, (t + 1) * bsz)
                xr, xi = (ar * xr - ai * xi + bu_ref[rows, re],
                          ar * xi + ai * xr + bu_ref[rows, im])
                x_ref[rows, re] = xr.astype(BF16)
                x_ref[rows, im] = xi.astype(BF16)
            st_ref[:, re] = xr
            st_ref[:, im] = xi

            cols = slice(piece * kw, (piece + 1) * kw)
            y = y + _dot(x_old_ref[:, cols], ct_ref[cols, :])
            bu_new_ref[:, cols] = _dot(u_new, bt_ref[:, cols])

        y = jax.nn.gelu(y)
        out = y * jax.nn.sigmoid(_dot(y.astype(BF16), wg_ref[...]))
        for h in range(planes):
            ot_ref[h] = out[:, h * LANES:(h + 1) * LANES]
        for b in range(bsz):
            for h in range(planes):
                chans = slice(b * S5_WIDTH + h * LANES, b * S5_WIDTH + (h + 1) * LANES)
                o_ref[:, chans] = ot_ref[h, pl.ds(b, steps, stride=bsz), :].astype(BF16)

    for slot in (0, 1):
        @pl.when(i % 2 == slot)
        def _():
            body(slot)


def _s5(u, bt, ar, ai, ct, d, wg, bsz, steps):
    L = u.shape[0]
    blk_rows = steps * bsz
    n_blocks = L // steps
    io_block = (steps, bsz * S5_WIDTH)
    const = lambda shape: pl.BlockSpec(shape, lambda t: (0, 0))
    return pl.pallas_call(
        functools.partial(_s5_kernel, steps=steps, bsz=bsz, slab=512),
        grid=(n_blocks + 2,),
        in_specs=[
            pl.BlockSpec(io_block, lambda t: (jnp.minimum(t, n_blocks - 1), 0)),
            pl.BlockSpec(io_block, lambda t: (jnp.maximum(t - 2, 0), 0)),
            const((S5_WIDTH, 2 * S5_NSTATE)),
            const((1, S5_NSTATE)), const((1, S5_NSTATE)),
            const((2 * S5_NSTATE, S5_WIDTH)),
            const((1, S5_WIDTH)),
            const((S5_WIDTH, S5_WIDTH)),
        ],
        out_specs=pl.BlockSpec(io_block, lambda t: (jnp.maximum(t - 2, 0), 0)),
        out_shape=jax.ShapeDtypeStruct((L, bsz * S5_WIDTH), BF16),
        scratch_shapes=[
            pltpu.VMEM((blk_rows, 2 * S5_NSTATE), F32), pltpu.VMEM((blk_rows, 2 * S5_NSTATE), F32),
            pltpu.VMEM((blk_rows, 2 * S5_NSTATE), BF16), pltpu.VMEM((blk_rows, 2 * S5_NSTATE), BF16),
            pltpu.VMEM((bsz, 2 * S5_NSTATE), F32),
            pltpu.VMEM((S5_WIDTH // LANES, blk_rows, LANES), F32),
            pltpu.VMEM((S5_WIDTH // LANES, blk_rows, LANES), F32),
            pltpu.VMEM((S5_WIDTH // LANES, blk_rows, LANES), F32),
        ],
        compiler_params=pltpu.CompilerParams(
            dimension_semantics=("arbitrary",), vmem_limit_bytes=VMEM_LIMIT_BYTES),
        name="s5_mixer",
    )(u, u, bt, ar, ai, ct, d, wg)


FF_TILE_STARTS = (0, 768, 1536, 2304, D_FF)
FINAL_ROW_TILE = 512


def _out_ffn_kernel(*refs, rows, parts, drop_prefix):
    n_act = 4 * parts
    h_refs, sb_refs, ro_refs, so_refs = (refs[i * parts:(i + 1) * parts] for i in range(4))
    wo_ref, g2_ref, wg_ref, wu_ref, wd_ref, o_ref = refs[n_act:]
    stack = lambda chunks: chunks[0] if parts == 1 else jnp.concatenate(chunks, axis=0)
    mix = jnp.concatenate([stack([r[0] for r in sb_refs]), stack([r[0] for r in ro_refs]),
                           stack([r[...] for r in so_refs])], axis=1)
    h1 = stack([r[0] for r in h_refs]) + _dot(mix, wo_ref[...])
    ms = jnp.mean(h1 * h1, axis=-1, keepdims=True)
    hn = (h1 * lax.rsqrt(ms + RMS_EPS) * g2_ref[...]).astype(BF16)
    ffn = jnp.zeros_like(h1)
    for lo, hi in zip(FF_TILE_STARTS[:-1], FF_TILE_STARTS[1:]):
        gate = _dot(hn, wg_ref[:, lo:hi])
        up = _dot(hn, wu_ref[:, lo:hi])
        act = (gate * jax.nn.sigmoid(gate)) * up
        ffn = ffn + _dot(act.astype(BF16), wd_ref[lo:hi, :])
    h2 = h1 + ffn
    if drop_prefix:
        o_ref[0] = h2
    else:
        pos = pl.program_id(1) * rows + lax.broadcasted_iota(jnp.int32, (rows, 1), 0)
        o_ref[0] = jnp.where(pos >= PAD_FRONT, h2, 0.0)


def _out_ffn(h, sb_o, ro, so_tm, w_out, g2, w_gate, w_up, w_down, tm, drop_prefix=False):
    bsz, L, _ = h.shape
    if drop_prefix:
        rows, parts, first_chunk = FINAL_ROW_TILE, FINAL_ROW_TILE // CHUNK, 1
        out_rows = L - CHUNK
    else:
        rows, parts, first_chunk = tm, 1, 0
        out_rows = L
    chunk = rows // parts
    act3 = lambda w: [pl.BlockSpec((1, chunk, w), lambda b, t, j=j: (b, first_chunk + parts * t + j, 0))
                      for j in range(parts)]
    so_specs = [pl.BlockSpec((chunk, S5_WIDTH), lambda b, t, j=j: (first_chunk + parts * t + j, b))
                for j in range(parts)]
    const = lambda shape: pl.BlockSpec(shape, lambda b, t: (0, 0), pipeline_mode=pl.Buffered(1))
    return pl.pallas_call(
        functools.partial(_out_ffn_kernel, rows=rows, parts=parts, drop_prefix=drop_prefix),
        grid=(bsz, out_rows // rows),
        in_specs=[
            *act3(D_MODEL), *act3(SB_WIDTH), *act3(RET_WIDTH), *so_specs,
            const((D_MODEL, D_MODEL)), const((1, D_MODEL)),
            const((D_MODEL, D_FF)), const((D_MODEL, D_FF)), const((D_FF, D_MODEL)),
        ],
        out_specs=pl.BlockSpec((1, rows, D_MODEL), lambda b, t: (b, t, 0)),
        out_shape=jax.ShapeDtypeStruct((bsz, out_rows, D_MODEL), F32),
        compiler_params=pltpu.CompilerParams(
            dimension_semantics=("arbitrary", "arbitrary"), vmem_limit_bytes=VMEM_LIMIT_BYTES),
        name="out_ffn",
    )(*([h] * parts), *([sb_o] * parts), *([ro] * parts), *([so_tm] * parts),
      w_out, g2, w_gate, w_up, w_down)


def _block_diag_mean(width):
    idx = jnp.arange(width) // HEAD_DIM
    return jnp.where(idx[:, None] == idx[None, :], 1.0 / HEAD_DIM, 0.0).astype(BF16)


def _suffix_count_matrix(tk):
    idx = jnp.arange(2 * tk)
    same = (idx[:, None] // tk) == (idx[None, :] // tk)
    return jnp.where(same & (idx[:, None] >= idx[None, :]), 1.0, 0.0).astype(BF16)


def _rope_tables(L):
    half = HEAD_DIM // 2
    pos = (jnp.arange(L) - PAD_FRONT).astype(F32)
    inv = ROPE_BASE ** (-jnp.arange(half, dtype=F32) / half)
    ang = pos[:, None] * inv[None, :]
    cos = jnp.cos(ang)
    sin = jnp.sin(ang)
    cos_t = jnp.tile(jnp.concatenate([cos, cos], axis=1), (1, LANES // HEAD_DIM))
    sin_t = jnp.tile(jnp.concatenate([-sin, sin], axis=1), (1, LANES // HEAD_DIM))
    return cos_t, sin_t


def _retention_tables():
    log_gamma = jnp.log1p(-jnp.exp2(-5.0 - jnp.arange(RET_HEADS, dtype=F32)))
    i = jnp.arange(CHUNK, dtype=F32)
    diff = i[:, None] - i[None, :]
    decay_in = jnp.where(diff >= 0, jnp.exp(log_gamma[:, None, None] * jnp.maximum(diff, 0.0)), 0.0)
    q_decay = jnp.exp(log_gamma[:, None] * (i + 1.0))
    k_decay = jnp.exp(log_gamma[:, None] * (CHUNK - 1.0 - i))
    chunk_decay = jnp.exp(log_gamma * CHUNK)
    per_lane = lambda t: jnp.repeat(t.T.reshape(CHUNK, HEAD_PAIRS, 2), HEAD_DIM, axis=2).transpose(1, 0, 2)
    dq = per_lane(q_decay)
    dk = per_lane(k_decay)
    din = decay_in.reshape(HEAD_PAIRS, 2, CHUNK, CHUNK).transpose(0, 2, 1, 3).reshape(HEAD_PAIRS, CHUNK, 2 * CHUNK)
    cd = jnp.repeat(chunk_decay.reshape(HEAD_PAIRS, 1, 2), HEAD_DIM, axis=2)
    return dq, dk, din, cd


def _s5_params(lam_re, lam_im, log_dt, b_re, b_im, c_re, c_im):
    dt = jnp.exp(log_dt)[:, None]
    mag = jnp.exp(lam_re * dt)
    ar = mag * jnp.cos(lam_im * dt)
    ai = mag * jnp.sin(lam_im * dt)
    den = lam_re * lam_re + lam_im * lam_im
    fr = ((ar - 1.0) * lam_re + ai * lam_im) / den
    fi = (ai * lam_re - (ar - 1.0) * lam_im) / den
    bbr = fr[..., None] * b_re - fi[..., None] * b_im
    bbi = fr[..., None] * b_im + fi[..., None] * b_re
    eye = jnp.eye(S5_GROUPS, dtype=F32)
    expand_b = lambda m: jnp.einsum('gpc,gh->gchp', m, eye).reshape(S5_WIDTH, S5_NSTATE)
    bt = jnp.concatenate([expand_b(bbr), expand_b(bbi)], axis=1).astype(BF16)
    expand_c = lambda m: jnp.einsum('gcp,gh->hpgc', m, eye).reshape(S5_NSTATE, S5_WIDTH)
    ct = jnp.concatenate([expand_c(c_re), -expand_c(c_im)], axis=0).astype(BF16)
    return bt, ar.reshape(1, S5_NSTATE), ai.reshape(1, S5_NSTATE), ct


def _largest_tile(L, candidates):
    for c in candidates:
        if L % c == 0:
            return c
    raise ValueError(f"no supported tile divides sequence length {L}")


def kernel(x, meta_tokens, norm1_g, w_in, sb_q_g, sb_k_g, ret_q_g, ret_k_g, ret_out_g, s5_lam_re, s5_lam_im,
           s5_log_dt, s5_b_re, s5_b_im, s5_c_re, s5_c_im, s5_d, s5_w_glu, w_out, norm2_g, w_gate, w_up, w_down):
    bsz, seq, _ = x.shape
    depth = w_in.shape[0]
    L = PAD_FRONT + N_META + seq
    assert L % CHUNK == 0
    tm = _largest_tile(L, (528, 384, 128))
    tq = _largest_tile(L, (384,))
    ret_chunks = _largest_tile(L // CHUNK, (11, 3, 1))
    s5_steps = _largest_tile(L, (32, 16, 8))

    prefix = jnp.concatenate([jnp.zeros((PAD_FRONT, D_MODEL), x.dtype), meta_tokens.astype(x.dtype)], axis=0)
    h = x

    bd128 = _block_diag_mean(LANES)
    bd256 = _block_diag_mean(2 * LANES)
    tri = _suffix_count_matrix(CHUNK)
    cos_t, sin_t = _rope_tables(L)
    dq, dk, din, cd = _retention_tables()
    tile6 = lambda g: jnp.tile(g, SB_HEADS)

    for l in range(depth):
        head_gains = jnp.stack([tile6(sb_q_g[l]), tile6(sb_k_g[l]), tile6(ret_q_g[l]), tile6(ret_k_g[l])])
        outs = _in_proj(h, norm1_g[l][None], w_in[l].astype(BF16), head_gains, bd256, cos_t, sin_t, tm,
                        prefix=prefix if l == 0 else None)
        if l == 0:
            h = outs[-1]
        sq, sk, sv, rq, rk, rv, rg, u = outs[:8]
        sb_o = _sb_attention(sq, sk, sv, tri, tq, CHUNK)
        ro = _retention(rq, rk, rv, rg, dq, dk, din, cd, ret_out_g[l][None], bd128, ret_chunks)
        bt, ar, ai, ct = _s5_params(s5_lam_re[l], s5_lam_im[l], s5_log_dt[l], s5_b_re[l], s5_b_im[l],
                                    s5_c_re[l], s5_c_im[l])
        so = _s5(u, bt, ar, ai, ct, s5_d[l][None], s5_w_glu[l].astype(BF16), bsz, s5_steps)
        h = _out_ffn(h, sb_o, ro, so, w_out[l].astype(BF16), norm2_g[l][None],
                     w_gate[l].astype(BF16), w_up[l].astype(BF16), w_down[l].astype(BF16), tm,
                     drop_prefix=(l == depth - 1 and seq % FINAL_ROW_TILE == 0))
    return h if h.shape[1] == seq else h[:, PAD_FRONT + N_META:]
```

```python
import functools
import math

import jax
import jax.numpy as jnp
from jax import lax
from jax.experimental import pallas as pl
from jax.experimental.pallas import tpu as pltpu

D_MODEL = 1024
HEAD_DIM = 64
SB_HEADS = 6
RET_HEADS = 6
SB_WIDTH = SB_HEADS * HEAD_DIM
RET_WIDTH = RET_HEADS * HEAD_DIM
S5_WIDTH = D_MODEL - SB_WIDTH - RET_WIDTH
S5_GROUP_CH = 16
S5_GROUPS = S5_WIDTH // S5_GROUP_CH
S5_STATE = 64
S5_NSTATE = S5_GROUPS * S5_STATE
D_IN = 3 * SB_WIDTH + 4 * RET_WIDTH + S5_WIDTH
D_FF = 2816
CHUNK = 128
N_META = 16
PAD_FRONT = CHUNK - N_META
ROPE_BASE = 10000.0
RMS_EPS = 1e-6
SB_QUERY_SCALE = HEAD_DIM ** -0.5 * math.log2(math.e)
S5_INTERLEAVE = 4
MASKED_LOG2_WEIGHT = -1e30

LANES = 128
HEAD_PAIRS = SB_HEADS // 2
VMEM_LIMIT_BYTES = 56 * 1024 * 1024

F32 = jnp.float32
BF16 = jnp.bfloat16


def _dot(a, b):
    return jnp.dot(a, b, preferred_element_type=F32)


def _dot_nt(a, b):
    return lax.dot_general(a, b, (((1,), (1,)), ((), ())), preferred_element_type=F32)


def _dot_tn(a, b):
    return lax.dot_general(a, b, (((0,), (0,)), ((), ())), preferred_element_type=F32)


def _split_dot(x, m):
    hi = x.astype(BF16)
    lo = (x - hi.astype(F32)).astype(BF16)
    return _dot(hi, m) + _dot(lo, m)


def _head_split(t, first_head):
    zero = jnp.zeros_like(t)
    return jnp.concatenate([jnp.where(first_head, t, zero), jnp.where(first_head, zero, t)], axis=0)


def _first_head_lanes():
    return lax.broadcasted_iota(jnp.int32, (1, LANES), 1) < HEAD_DIM


TOKEN_TILE = 3 * CHUNK


def _in_kernel(*refs, from_tokens):
    if from_tokens:
        parts = TOKEN_TILE // CHUNK
        x_refs, prefix_ref, refs = refs[:parts], refs[parts], refs[parts + 1:]
        head = jnp.where(pl.program_id(1) == 0, prefix_ref[...], x_refs[0][0])
        x = jnp.concatenate([head] + [r[0] for r in x_refs[1:]], axis=0)
        refs[-1][0] = x
        refs = refs[:-1]
    else:
        x = refs[0][0]
        refs = refs[1:]
    (g1_ref, w_ref, hg_ref, bd_ref, cos_ref, sin_ref,
     sq_ref, sk_ref, sv_ref, rq_ref, rk_ref, rv_ref, rg_ref, u_ref) = refs
    ms = jnp.mean(x * x, axis=-1, keepdims=True)
    hn = (x * lax.rsqrt(ms + RMS_EPS) * g1_ref[...]).astype(BF16)

    def proj_pair(seg):
        p = _dot(hn, w_ref[:, seg * SB_WIDTH:min((seg + 2) * SB_WIDTH, D_IN)])
        return p[:, :SB_WIDTH], p[:, SB_WIDTH:]

    def head_norm_pair(a, b, gi):
        p = jnp.concatenate([a, b], axis=1)
        sq = (p * p).astype(BF16)
        wide = bd_ref.shape[0]
        m = jnp.concatenate(
            [_dot(sq[:, c:c + wide], bd_ref[...]) for c in range(0, 2 * SB_WIDTH, wide)], axis=1)
        gains = jnp.concatenate([hg_ref[gi:gi + 1, :], hg_ref[gi + 1:gi + 2, :]], axis=1)
        p = p * lax.rsqrt(m + RMS_EPS) * gains
        return p[:, :SB_WIDTH], p[:, SB_WIDTH:]

    lane = lax.broadcasted_iota(jnp.int32, (1, LANES), 1)
    lower = (lane & (HEAD_DIM - 1)) < HEAD_DIM // 2
    cos = cos_ref[...]
    sin = sin_ref[...]

    def rotary(t):
        outs = []
        for c in range(HEAD_PAIRS):
            tc = t[:, c * LANES:(c + 1) * LANES]
            partner = jnp.where(lower, pltpu.roll(tc, LANES - HEAD_DIM // 2, 1),
                                pltpu.roll(tc, HEAD_DIM // 2, 1))
            outs.append(tc * cos + partner * sin)
        return jnp.concatenate(outs, axis=1)

    wide_lane = lax.broadcasted_iota(jnp.int32, (1, SB_WIDTH), 1)
    first_head = (wide_lane & (LANES - 1)) < HEAD_DIM

    def store_per_head(ref, t):
        t = t.astype(BF16)
        zero = jnp.zeros_like(t)
        ref[0, 0] = jnp.where(first_head, t, zero)
        ref[0, 1] = jnp.where(first_head, zero, t)

    sq, sk = head_norm_pair(*proj_pair(0), 0)
    sq_ref[0] = (sq * SB_QUERY_SCALE).astype(BF16)
    store_per_head(sk_ref, sk)
    sv, rq = proj_pair(2)
    store_per_head(sv_ref, sv)
    rk, rv = proj_pair(4)
    rv_ref[0] = rv.astype(BF16)
    rq, rk = head_norm_pair(rq, rk, 2)
    rq_ref[0] = rotary(rq)
    rk_ref[0] = rotary(rk)
    rg_ref[0], u_ref[...] = proj_pair(6)


def _in_proj(h, g1, w_in, head_gains, bd, cos_t, sin_t, tm, prefix=None):
    from_tokens = prefix is not None
    bsz = h.shape[0]
    if from_tokens:
        tm = TOKEN_TILE
        L = h.shape[1] + CHUNK
        parts = tm // CHUNK
        act_specs = [pl.BlockSpec((1, CHUNK, D_MODEL), lambda b, t, j=j: (b, jnp.maximum(parts * t - 1 + j, 0), 0))
                     for j in range(parts)]
        act_specs.append(pl.BlockSpec((CHUNK, D_MODEL), lambda b, t: (0, 0)))
        acts = [h] * parts + [prefix]
    else:
        L = h.shape[1]
        act_specs = [pl.BlockSpec((1, tm, D_MODEL), lambda b, t: (b, t, 0))]
        acts = [h]
    grid = (bsz, L // tm)
    row3 = lambda w: pl.BlockSpec((1, tm, w), lambda b, t: (b, t, 0))
    const = lambda shape: pl.BlockSpec(shape, lambda b, t: (0,) * len(shape), pipeline_mode=pl.Buffered(1))
    out_shapes = [
        jax.ShapeDtypeStruct((bsz, L, SB_WIDTH), BF16),
        jax.ShapeDtypeStruct((bsz, 2, L, SB_WIDTH), BF16),
        jax.ShapeDtypeStruct((bsz, 2, L, SB_WIDTH), BF16),
        jax.ShapeDtypeStruct((bsz, L, RET_WIDTH), F32),
        jax.ShapeDtypeStruct((bsz, L, RET_WIDTH), F32),
        jax.ShapeDtypeStruct((bsz, L, RET_WIDTH), BF16),
        jax.ShapeDtypeStruct((bsz, L, RET_WIDTH), F32),
        jax.ShapeDtypeStruct((L, bsz * S5_WIDTH), F32),
    ]
    per_head = pl.BlockSpec((1, 2, tm, SB_WIDTH), lambda b, t: (b, 0, t, 0))
    out_specs = ([row3(SB_WIDTH), per_head, per_head] + [row3(RET_WIDTH)] * 4
                 + [pl.BlockSpec((tm, S5_WIDTH), lambda b, t: (t, b))])
    if from_tokens:
        out_shapes.append(jax.ShapeDtypeStruct((bsz, L, D_MODEL), F32))
        out_specs.append(row3(D_MODEL))
    return pl.pallas_call(
        functools.partial(_in_kernel, from_tokens=from_tokens),
        grid=grid,
        in_specs=[
            *act_specs,
            const((1, D_MODEL)),
            const((D_MODEL, D_IN)),
            const((4, SB_WIDTH)),
            const(bd.shape),
            pl.BlockSpec((tm, LANES), lambda b, t: (t, 0)),
            pl.BlockSpec((tm, LANES), lambda b, t: (t, 0)),
        ],
        out_specs=out_specs,
        out_shape=out_shapes,
        compiler_params=pltpu.CompilerParams(
            dimension_semantics=("arbitrary", "arbitrary"), vmem_limit_bytes=VMEM_LIMIT_BYTES),
        name="in_proj",
    )(*acts, g1, w_in, head_gains, bd, cos_t, sin_t)


LAST_SLOT = 2


def _sb_kernel(q_ref, k_ref, v_ref, tri_ref, o_ref, acc_ref, c_ref, accp_ref, cp_ref, sp_ref, z_ref, v0_ref,
               *, tq, tk):
    qi = pl.program_id(1)
    n_q_blocks = pl.num_programs(1) - 1
    n_key_blocks = k_ref.shape[2] // tk
    width = 2 * tk
    t_idx = qi * tq + lax.broadcasted_iota(jnp.int32, (tq, 1), 0)
    s_lane = lax.broadcasted_iota(jnp.int32, (1, width), 1) & (tk - 1)

    n_pairs = ((qi + 1) * (tq // tk) + 1) // 2

    def block_rows(j):
        return pl.ds(pl.multiple_of(jnp.minimum(j, n_key_blocks - 1) * tk, tk), tk)

    def lanes(p):
        return slice(p * LANES, (p + 1) * LANES)

    def per_head_rows(ref, j, p):
        return jnp.concatenate([ref[0, 0, block_rows(j), lanes(p)], ref[0, 1, block_rows(j), lanes(p)]], axis=0)

    def score_dots(m, p):
        q = q_ref[0, :, lanes(p)]
        return [_dot_nt(q, per_head_rows(k_ref, j, p)) for j in (2 * m + 1, 2 * m)]

    def score_logs(zs, m, p, slot, causal, padded):
        for half, (z, j) in enumerate(zip(zs, (2 * m + 1, 2 * m))):
            if causal or padded:
                s_idx = j * tk + s_lane
                mask = s_idx < t_idx if causal else s_idx >= PAD_FRONT
                if causal and padded:
                    mask = mask & (s_idx >= PAD_FRONT)
                z = jnp.where(mask, z, MASKED_LOG2_WEIGHT)
            neg_abs = lax.bitcast_convert_type(
                lax.bitcast_convert_type(z, jnp.uint32) | jnp.uint32(0x80000000), F32)
            sp = jnp.maximum(z, 0.0) + jnp.log2(1.0 + jnp.exp2(neg_abs))
            cols = slice(half * width, (half + 1) * width)
            sp_ref[p, slot, :, cols] = sp.astype(BF16)
            z_ref[p, slot, :, cols] = z

    def weights(slot, p, c_in_ref, c_out_ref):
        c = c_in_ref[p]
        ws = []
        for half in range(2):
            cols = slice(half * width, (half + 1) * width)
            later = _dot(sp_ref[p, slot, :, cols], tri_ref[...]) + c
            ws.append(jnp.exp2(z_ref[p, slot, :, cols] - later).astype(BF16))
            c = jnp.concatenate([jnp.broadcast_to(later[:, 0:1], (tq, tk)),
                                 jnp.broadcast_to(later[:, tk:tk + 1], (tq, tk))], axis=1)
        if c_out_ref is not None:
            c_out_ref[p] = c
        return jnp.concatenate(ws, axis=1)

    def weighted_values(w, m, p):
        vv = jnp.concatenate([per_head_rows(v_ref, j, p) for j in (2 * m + 1, 2 * m)], axis=0)
        return _dot(w, vv)

    def weighted_values_pair0(w, p):
        vv = jnp.concatenate([v0_ref[head, rows, lanes(p)]
                              for rows in (slice(tk, 2 * tk), slice(0, tk)) for head in (0, 1)], axis=0)
        return _dot(w, vv)

    def step(m, slot, causal, padded, last=False):
        weight_slot = 1 if last else 1 - slot
        c_out_ref, acc_out_ref = (cp_ref, accp_ref) if last else (c_ref, acc_ref)
        for p in range(HEAD_PAIRS):
            score_logs(score_dots(m, p), m, p, slot, causal, padded)
            w = weights(weight_slot, p, c_ref, c_out_ref)
            acc_out_ref[:, lanes(p)] = acc_ref[:, lanes(p)] + weighted_values(w, m + 1, p)
        if last:
            v0_ref[...] = v_ref[0, :, 0:2 * tk, :]

    def finish_previous(p):
        w = weights(LAST_SLOT, p, cp_ref, None)
        o_ref[0, :, lanes(p)] = (accp_ref[:, lanes(p)] + weighted_values_pair0(w, p)).astype(BF16)

    @pl.when(qi == 0)
    def _():
        acc_ref[...] = jnp.zeros_like(acc_ref)
        c_ref[...] = jnp.zeros_like(c_ref)
        for p in range(HEAD_PAIRS):
            score_logs(score_dots(1, p), 1, p, 1, True, False)
        step(0, LAST_SLOT, True, True, last=True)

    @pl.when(qi == n_q_blocks)
    def _():
        for p in range(HEAD_PAIRS):
            finish_previous(p)

    @pl.when(jnp.logical_and(qi > 0, qi < n_q_blocks))
    def _():
        for parity in (0, 1):
            @pl.when((n_pairs - 1) % 2 == parity)
            def _():
                acc_ref[...] = jnp.zeros_like(acc_ref)
                c_ref[...] = jnp.zeros_like(c_ref)
                for p in range(HEAD_PAIRS):
                    finish_previous(p)
                    score_logs(score_dots(n_pairs - 1, p), n_pairs - 1, p, parity, True, False)
                step(n_pairs - 2, 1 - parity, True, False)

        @pl.when(n_pairs % 2 == 0)
        def _():
            step(n_pairs - 3, 1, False, False)

        n_double = (n_pairs - 3) // 2

        def body(i, carry):
            m = 2 * (n_double - i)
            step(m, 0, False, False)
            step(m - 1, 1, False, False)
            return carry

        lax.fori_loop(0, n_double, body, 0)
        step(0, LAST_SLOT, False, True, last=True)


def _sb_attention(q, k, v, tri, tq, tk):
    bsz, L, _ = q.shape
    assert tq % tk == 0 and tq // tk >= 3, "the pipeline prologue needs two key-block pairs per query block"
    n_q_blocks = L // tq
    slot_shape = (HEAD_PAIRS, 3, tq, 4 * tk)

    def kv_index(b, i):
        return (jnp.where(i == n_q_blocks, jnp.minimum(b + 1, bsz - 1), b), 0, 0, 0)

    return pl.pallas_call(
        functools.partial(_sb_kernel, tq=tq, tk=tk),
        grid=(bsz, n_q_blocks + 1),
        in_specs=[
            pl.BlockSpec((1, tq, SB_WIDTH), lambda b, i: (b, jnp.minimum(i, n_q_blocks - 1), 0)),
            pl.BlockSpec((1, 2, L, SB_WIDTH), kv_index), pl.BlockSpec((1, 2, L, SB_WIDTH), kv_index),
            pl.BlockSpec((2 * tk, 2 * tk), lambda b, i: (0, 0)),
        ],
        out_specs=pl.BlockSpec((1, tq, SB_WIDTH), lambda b, i: (b, jnp.maximum(i - 1, 0), 0)),
        out_shape=jax.ShapeDtypeStruct((bsz, L, SB_WIDTH), BF16),
        scratch_shapes=[pltpu.VMEM((tq, SB_WIDTH), F32), pltpu.VMEM((HEAD_PAIRS, tq, 2 * tk), F32),
                        pltpu.VMEM((tq, SB_WIDTH), F32), pltpu.VMEM((HEAD_PAIRS, tq, 2 * tk), F32),
                        pltpu.VMEM(slot_shape, BF16), pltpu.VMEM(slot_shape, F32),
                        pltpu.VMEM((2, 2 * tk, SB_WIDTH), BF16)],
        compiler_params=pltpu.CompilerParams(
            dimension_semantics=("arbitrary", "arbitrary"), vmem_limit_bytes=VMEM_LIMIT_BYTES),
        name="sb_attention",
    )(q, k, v, tri)


def _ret_kernel(q_ref, k_ref, v_ref, g_ref, dq_ref, dk_ref, din_ref, cd_ref, go_ref, bd_ref,
                o_ref, s_ref, *, chunks):
    @pl.when(pl.program_id(1) == 0)
    def _():
        s_ref[...] = jnp.zeros_like(s_ref)

    first = _first_head_lanes()
    same_head = (lax.broadcasted_iota(jnp.int32, (LANES, 1), 0) < HEAD_DIM) == first
    units = [(c, p) for c in range(chunks) for p in range(HEAD_PAIRS)]

    def block(ref, c, p):
        return ref[0, c * CHUNK:(c + 1) * CHUNK, p * LANES:(p + 1) * LANES]

    q = {u: block(q_ref, *u) for u in units}
    k = {u: block(k_ref, *u) * (HEAD_DIM ** -0.5) for u in units}
    v = {u: block(v_ref, *u) for u in units}
    scores = {(c, p): _dot_nt(q[c, p].astype(BF16), _head_split(k[c, p].astype(BF16), first)) * din_ref[p]
              for c, p in units}
    inner = {u: _dot(scores[u].astype(BF16), _head_split(v[u], first)) for u in units}
    upd = {(c, p): _dot_tn((k[c, p] * dk_ref[p]).astype(BF16), v[c, p]) for c, p in units}
    out = {}
    for p in range(HEAD_PAIRS):
        state = s_ref[p]
        for c in range(chunks):
            cross = _dot((q[c, p] * dq_ref[p]).astype(BF16), state.astype(BF16))
            state = state * cd_ref[p] + jnp.where(same_head, upd[c, p], 0.0)
            out[c, p] = inner[c, p] + cross
        s_ref[p] = state
    for c, p in units:
        o = out[c, p]
        ms = _split_dot(o * o, bd_ref[...])
        on = o * lax.rsqrt(ms + RMS_EPS) * go_ref[:, p * LANES:(p + 1) * LANES]
        g = block(g_ref, c, p)
        o_ref[0, c * CHUNK:(c + 1) * CHUNK, p * LANES:(p + 1) * LANES] = (
            on * (g * jax.nn.sigmoid(g))).astype(BF16)


def _retention(q, k, v, g, dq, dk, din, cd, gout, bd, chunks):
    bsz, L, _ = q.shape
    rows = chunks * CHUNK
    grid = (bsz, L // rows)
    blk = pl.BlockSpec((1, rows, RET_WIDTH), lambda b, n: (b, n, 0))
    const = lambda shape: pl.BlockSpec(shape, lambda b, n: (0,) * len(shape))
    return pl.pallas_call(
        functools.partial(_ret_kernel, chunks=chunks),
        grid=grid,
        in_specs=[
            blk, blk, blk, blk,
            const((HEAD_PAIRS, CHUNK, LANES)), const((HEAD_PAIRS, CHUNK, LANES)),
            const((HEAD_PAIRS, CHUNK, 2 * CHUNK)), const((HEAD_PAIRS, 1, LANES)),
            const((1, RET_WIDTH)), const((LANES, LANES)),
        ],
        out_specs=blk,
        out_shape=jax.ShapeDtypeStruct((bsz, L, RET_WIDTH), BF16),
        scratch_shapes=[pltpu.VMEM((HEAD_PAIRS, LANES, LANES), F32)],
        compiler_params=pltpu.CompilerParams(
            dimension_semantics=("arbitrary", "arbitrary"), vmem_limit_bytes=VMEM_LIMIT_BYTES),
        name="retention",
    )(q, k, v, g, dq, dk, din, cd, gout, bd)


def _s5_kernel(u_ref, u_old_ref, bt_ref, ar_ref, ai_ref, ct_ref, d_ref, wg_ref, o_ref,
               bu0_ref, bu1_ref, x0_ref, x1_ref, st_ref, ut_ref, uo_ref, ot_ref, *, steps, bsz, slab):
    i = pl.program_id(0)
    bu_refs = (bu0_ref, bu1_ref)
    x_refs = (x0_ref, x1_ref)

    @pl.when(i == 0)
    def _():
        for ref in (st_ref, bu0_ref, bu1_ref, x0_ref, x1_ref):
            ref[...] = jnp.zeros_like(ref)

    def body(slot):
        bu_new_ref, bu_ref = bu_refs[slot], bu_refs[1 - slot]
        x_old_ref, x_ref = x_refs[slot], x_refs[1 - slot]
        planes = S5_WIDTH // LANES
        for b in range(bsz):
            for h in range(planes):
                chans = slice(b * S5_WIDTH + h * LANES, b * S5_WIDTH + (h + 1) * LANES)
                ut_ref[h, pl.ds(b, steps, stride=bsz), :] = u_ref[:, chans]
                uo_ref[h, pl.ds(b, steps, stride=bsz), :] = u_old_ref[:, chans]
        u_old = jnp.concatenate([uo_ref[h] for h in range(planes)], axis=1)
        u_new = jnp.concatenate([ut_ref[h] for h in range(planes)], axis=1).astype(BF16)
        y = d_ref[...] * u_old
        slabs = S5_NSTATE // slab
        runs_per_slab = S5_INTERLEAVE // slabs
        run = steps // runs_per_slab
        kw = 2 * S5_NSTATE // S5_INTERLEAVE
        for piece in range(S5_INTERLEAVE):
            ls, part = divmod(piece, runs_per_slab)
            re = slice(ls * slab, (ls + 1) * slab)
            im = slice(S5_NSTATE + ls * slab, S5_NSTATE + (ls + 1) * slab)
            ar = jnp.broadcast_to(ar_ref[:, re], (bsz, slab))
            ai = jnp.broadcast_to(ai_ref[:, re], (bsz, slab))
            xr = st_ref[:, re]
            xi = st_ref[:, im]
            for t in range(part * run, (part + 1) * run):
                rows = slice(t * bsz, (t + 1) * bsz)
                xr, xi = (ar * xr - ai * xi + bu_ref[rows, re],
                          ar * xi + ai * xr + bu_ref[rows, im])
                x_ref[rows, re] = xr.astype(BF16)
                x_ref[rows, im] = xi.astype(BF16)
            st_ref[:, re] = xr
            st_ref[:, im] = xi

            cols = slice(piece * kw, (piece + 1) * kw)
            y = y + _dot(x_old_ref[:, cols], ct_ref[cols, :])
            bu_new_ref[:, cols] = _dot(u_new, bt_ref[:, cols])

        y = jax.nn.gelu(y)
        out = y * jax.nn.sigmoid(_dot(y.astype(BF16), wg_ref[...]))
        for h in range(planes):
            ot_ref[h] = out[:, h * LANES:(h + 1) * LANES]
        for b in range(bsz):
            for h in range(planes):
                chans = slice(b * S5_WIDTH + h * LANES, b * S5_WIDTH + (h + 1) * LANES)
                o_ref[:, chans] = ot_ref[h, pl.ds(b, steps, stride=bsz), :].astype(BF16)

    for slot in (0, 1):
        @pl.when(i % 2 == slot)
        def _():
            body(slot)


def _s5(u, bt, ar, ai, ct, d, wg, bsz, steps):
    L = u.shape[0]
    blk_rows = steps * bsz
    n_blocks = L // steps
    io_block = (steps, bsz * S5_WIDTH)
    const = lambda shape: pl.BlockSpec(shape, lambda t: (0, 0))
    return pl.pallas_call(
        functools.partial(_s5_kernel, steps=steps, bsz=bsz, slab=512),
        grid=(n_blocks + 2,),
        in_specs=[
            pl.BlockSpec(io_block, lambda t: (jnp.minimum(t, n_blocks - 1), 0)),
            pl.BlockSpec(io_block, lambda t: (jnp.maximum(t - 2, 0), 0)),
            const((S5_WIDTH, 2 * S5_NSTATE)),
            const((1, S5_NSTATE)), const((1, S5_NSTATE)),
            const((2 * S5_NSTATE, S5_WIDTH)),
            const((1, S5_WIDTH)),
            const((S5_WIDTH, S5_WIDTH)),
        ],
        out_specs=pl.BlockSpec(io_block, lambda t: (jnp.maximum(t - 2, 0), 0)),
        out_shape=jax.ShapeDtypeStruct((L, bsz * S5_WIDTH), BF16),
        scratch_shapes=[
            pltpu.VMEM((blk_rows, 2 * S5_NSTATE), F32), pltpu.VMEM((blk_rows, 2 * S5_NSTATE), F32),
            pltpu.VMEM((blk_rows, 2 * S5_NSTATE), BF16), pltpu.VMEM((blk_rows, 2 * S5_NSTATE), BF16),
            pltpu.VMEM((bsz, 2 * S5_NSTATE), F32),
            pltpu.VMEM((S5_WIDTH // LANES, blk_rows, LANES), F32),
            pltpu.VMEM((S5_WIDTH // LANES, blk_rows, LANES), F32),
            pltpu.VMEM((S5_WIDTH // LANES, blk_rows, LANES), F32),
        ],
        compiler_params=pltpu.CompilerParams(
            dimension_semantics=("arbitrary",), vmem_limit_bytes=VMEM_LIMIT_BYTES),
        name="s5_mixer",
    )(u, u, bt, ar, ai, ct, d, wg)


FF_TILE_STARTS = (0, 768, 1536, 2304, D_FF)
FINAL_ROW_TILE = 512


def _out_ffn_kernel(*refs, rows, parts, drop_prefix):
    n_act = 4 * parts
    h_refs, sb_refs, ro_refs, so_refs = (refs[i * parts:(i + 1) * parts] for i in range(4))
    wo_ref, g2_ref, wg_ref, wu_ref, wd_ref, o_ref = refs[n_act:]
    stack = lambda chunks: chunks[0] if parts == 1 else jnp.concatenate(chunks, axis=0)
    mix = jnp.concatenate([stack([r[0] for r in sb_refs]), stack([r[0] for r in ro_refs]),
                           stack([r[...] for r in so_refs])], axis=1)
    h1 = stack([r[0] for r in h_refs]) + _dot(mix, wo_ref[...])
    ms = jnp.mean(h1 * h1, axis=-1, keepdims=True)
    hn = (h1 * lax.rsqrt(ms + RMS_EPS) * g2_ref[...]).astype(BF16)
    ffn = jnp.zeros_like(h1)
    for lo, hi in zip(FF_TILE_STARTS[:-1], FF_TILE_STARTS[1:]):
        gate = _dot(hn, wg_ref[:, lo:hi])
        up = _dot(hn, wu_ref[:, lo:hi])
        act = (gate * jax.nn.sigmoid(gate)) * up
        ffn = ffn + _dot(act.astype(BF16), wd_ref[lo:hi, :])
    h2 = h1 + ffn
    if drop_prefix:
        o_ref[0] = h2
    else:
        pos = pl.program_id(1) * rows + lax.broadcasted_iota(jnp.int32, (rows, 1), 0)
        o_ref[0] = jnp.where(pos >= PAD_FRONT, h2, 0.0)


def _out_ffn(h, sb_o, ro, so_tm, w_out, g2, w_gate, w_up, w_down, tm, drop_prefix=False):
    bsz, L, _ = h.shape
    if drop_prefix:
        rows, parts, first_chunk = FINAL_ROW_TILE, FINAL_ROW_TILE // CHUNK, 1
        out_rows = L - CHUNK
    else:
        rows, parts, first_chunk = tm, 1, 0
        out_rows = L
    chunk = rows // parts
    act3 = lambda w: [pl.BlockSpec((1, chunk, w), lambda b, t, j=j: (b, first_chunk + parts * t + j, 0))
                      for j in range(parts)]
    so_specs = [pl.BlockSpec((chunk, S5_WIDTH), lambda b, t, j=j: (first_chunk + parts * t + j, b))
                for j in range(parts)]
    const = lambda shape: pl.BlockSpec(shape, lambda b, t: (0, 0), pipeline_mode=pl.Buffered(1))
    return pl.pallas_call(
        functools.partial(_out_ffn_kernel, rows=rows, parts=parts, drop_prefix=drop_prefix),
        grid=(bsz, out_rows // rows),
        in_specs=[
            *act3(D_MODEL), *act3(SB_WIDTH), *act3(RET_WIDTH), *so_specs,
            const((D_MODEL, D_MODEL)), const((1, D_MODEL)),
            const((D_MODEL, D_FF)), const((D_MODEL, D_FF)), const((D_FF, D_MODEL)),
        ],
        out_specs=pl.BlockSpec((1, rows, D_MODEL), lambda b, t: (b, t, 0)),
        out_shape=jax.ShapeDtypeStruct((bsz, out_rows, D_MODEL), F32),
        compiler_params=pltpu.CompilerParams(
            dimension_semantics=("arbitrary", "arbitrary"), vmem_limit_bytes=VMEM_LIMIT_BYTES),
        name="out_ffn",
    )(*([h] * parts), *([sb_o] * parts), *([ro] * parts), *([so_tm] * parts),
      w_out, g2, w_gate, w_up, w_down)


def _block_diag_mean(width):
    idx = jnp.arange(width) // HEAD_DIM
    return jnp.where(idx[:, None] == idx[None, :], 1.0 / HEAD_DIM, 0.0).astype(BF16)


def _suffix_count_matrix(tk):
    idx = jnp.arange(2 * tk)
    same = (idx[:, None] // tk) == (idx[None, :] // tk)
    return jnp.where(same & (idx[:, None] >= idx[None, :]), 1.0, 0.0).astype(BF16)


def _rope_tables(L):
    half = HEAD_DIM // 2
    pos = (jnp.arange(L) - PAD_FRONT).astype(F32)
    inv = ROPE_BASE ** (-jnp.arange(half, dtype=F32) / half)
    ang = pos[:, None] * inv[None, :]
    cos = jnp.cos(ang)
    sin = jnp.sin(ang)
    cos_t = jnp.tile(jnp.concatenate([cos, cos], axis=1), (1, LANES // HEAD_DIM))
    sin_t = jnp.tile(jnp.concatenate([-sin, sin], axis=1), (1, LANES // HEAD_DIM))
    return cos_t, sin_t


def _retention_tables():
    log_gamma = jnp.log1p(-jnp.exp2(-5.0 - jnp.arange(RET_HEADS, dtype=F32)))
    i = jnp.arange(CHUNK, dtype=F32)
    diff = i[:, None] - i[None, :]
    decay_in = jnp.where(diff >= 0, jnp.exp(log_gamma[:, None, None] * jnp.maximum(diff, 0.0)), 0.0)
    q_decay = jnp.exp(log_gamma[:, None] * (i + 1.0))
    k_decay = jnp.exp(log_gamma[:, None] * (CHUNK - 1.0 - i))
    chunk_decay = jnp.exp(log_gamma * CHUNK)
    per_lane = lambda t: jnp.repeat(t.T.reshape(CHUNK, HEAD_PAIRS, 2), HEAD_DIM, axis=2).transpose(1, 0, 2)
    dq = per_lane(q_decay)
    dk = per_lane(k_decay)
    din = decay_in.reshape(HEAD_PAIRS, 2, CHUNK, CHUNK).transpose(0, 2, 1, 3).reshape(HEAD_PAIRS, CHUNK, 2 * CHUNK)
    cd = jnp.repeat(chunk_decay.reshape(HEAD_PAIRS, 1, 2), HEAD_DIM, axis=2)
    return dq, dk, din, cd


def _s5_params(lam_re, lam_im, log_dt, b_re, b_im, c_re, c_im):
    dt = jnp.exp(log_dt)[:, None]
    mag = jnp.exp(lam_re * dt)
    ar = mag * jnp.cos(lam_im * dt)
    ai = mag * jnp.sin(lam_im * dt)
    den = lam_re * lam_re + lam_im * lam_im
    fr = ((ar - 1.0) * lam_re + ai * lam_im) / den
    fi = (ai * lam_re - (ar - 1.0) * lam_im) / den
    bbr = fr[..., None] * b_re - fi[..., None] * b_im
    bbi = fr[..., None] * b_im + fi[..., None] * b_re
    eye = jnp.eye(S5_GROUPS, dtype=F32)
    expand_b = lambda m: jnp.einsum('gpc,gh->gchp', m, eye).reshape(S5_WIDTH, S5_NSTATE)
    bt = jnp.concatenate([expand_b(bbr), expand_b(bbi)], axis=1).astype(BF16)
    expand_c = lambda m: jnp.einsum('gcp,gh->hpgc', m, eye).reshape(S5_NSTATE, S5_WIDTH)
    ct = jnp.concatenate([expand_c(c_re), -expand_c(c_im)], axis=0).astype(BF16)
    return bt, ar.reshape(1, S5_NSTATE), ai.reshape(1, S5_NSTATE), ct


def _largest_tile(L, candidates):
    for c in candidates:
        if L % c == 0:
            return c
    raise ValueError(f"no supported tile divides sequence length {L}")


def kernel(x, meta_tokens, norm1_g, w_in, sb_q_g, sb_k_g, ret_q_g, ret_k_g, ret_out_g, s5_lam_re, s5_lam_im,
           s5_log_dt, s5_b_re, s5_b_im, s5_c_re, s5_c_im, s5_d, s5_w_glu, w_out, norm2_g, w_gate, w_up, w_down):
    bsz, seq, _ = x.shape
    depth = w_in.shape[0]
    L = PAD_FRONT + N_META + seq
    assert L % CHUNK == 0
    tm = _largest_tile(L, (528, 384, 128))
    tq = _largest_tile(L, (384,))
    ret_chunks = _largest_tile(L // CHUNK, (11, 3, 1))
    s5_steps = _largest_tile(L, (64, 32, 16, 8))

    prefix = jnp.concatenate([jnp.zeros((PAD_FRONT, D_MODEL), x.dtype), meta_tokens.astype(x.dtype)], axis=0)
    h = x

    bd128 = _block_diag_mean(LANES)
    bd256 = _block_diag_mean(2 * LANES)
    tri = _suffix_count_matrix(CHUNK)
    cos_t, sin_t = _rope_tables(L)
    dq, dk, din, cd = _retention_tables()
    tile6 = lambda g: jnp.tile(g, SB_HEADS)

    for l in range(depth):
        head_gains = jnp.stack([tile6(sb_q_g[l]), tile6(sb_k_g[l]), tile6(ret_q_g[l]), tile6(ret_k_g[l])])
        outs = _in_proj(h, norm1_g[l][None], w_in[l].astype(BF16), head_gains, bd256, cos_t, sin_t, tm,
                        prefix=prefix if l == 0 else None)
        if l == 0:
            h = outs[-1]
        sq, sk, sv, rq, rk, rv, rg, u = outs[:8]
        sb_o = _sb_attention(sq, sk, sv, tri, tq, CHUNK)
        ro = _retention(rq, rk, rv, rg, dq, dk, din, cd, ret_out_g[l][None], bd128, ret_chunks)
        bt, ar, ai, ct = _s5_params(s5_lam_re[l], s5_lam_im[l], s5_log_dt[l], s5_b_re[l], s5_b_im[l],
                                    s5_c_re[l], s5_c_im[l])
        so = _s5(u, bt, ar, ai, ct, s5_d[l][None], s5_w_glu[l].astype(BF16), bsz, s5_steps)
        h = _out_ffn(h, sb_o, ro, so, w_out[l].astype(BF16), norm2_g[l][None],
                     w_gate[l].astype(BF16), w_up[l].astype(BF16), w_down[l].astype(BF16), tm,
                     drop_prefix=(l == depth - 1 and seq % FINAL_ROW_TILE == 0))
    return h if h.shape[1] == seq else h[:, PAD_FRONT + N_META:]
```
